```python
import math
import jax, jax.numpy as jnp
from jax import lax
import numpy as np

D_MODEL = 1024
BATCH = 16
SEQ = 4096
DEPTH = 2

GRID_W = 64
CTX_LEN = 256
D_MIX = D_MODEL
D_ATT = D_MIX // 2
D_CONV = D_MIX - D_ATT
N_HEADS = 4
DV = D_ATT // N_HEADS
DQK = DV // 2
N_FREQ = DQK // 4
ROPE_THETA = 10000.0
Q_BLOCK = 128
N_EXPERTS = 16
N_GROUPS = 4
EXPERTS_PER_GROUP = N_EXPERTS // N_GROUPS
TOP_K = 2
D_EXPERT = 512
LN_EPS = 1e-6
HEAD_NORM_EPS = 1e-5
ALPHA = (2 * DEPTH) ** 0.25
BETA = (8 * DEPTH) ** -0.25

kernel_name = 'hybrid_diffattn_shortconv_grouped_moe_dit'


def layer_norm(x, g=None, b=None):
    xf = x.astype(jnp.float32)
    mu = jnp.mean(xf, axis=-1, keepdims=True)
    var = jnp.mean(jnp.square(xf - mu), axis=-1, keepdims=True)
    y = ((xf - mu) * lax.rsqrt(var + LN_EPS)).astype(x.dtype)
    if g is not None:
        y = y * g + b
    return y


def modulate(x, shift, scale):
    return layer_norm(x) * (1 + scale) + shift


def rope_tables(n):
    t = jnp.arange(n, dtype=jnp.int32)
    row = (t // GRID_W).astype(jnp.float32)
    col = (t % GRID_W).astype(jnp.float32)
    inv = 1.0 / (ROPE_THETA ** (jnp.arange(N_FREQ, dtype=jnp.float32) / N_FREQ))
    ang = jnp.stack([row[:, None] * inv, col[:, None] * inv], axis=1)
    return jnp.cos(ang), jnp.sin(ang)


def apply_rope(x, cos, sin):
    xr = x.reshape(x.shape[:-1] + (2, 2, N_FREQ))
    x1, x2 = xr[..., 0, :], xr[..., 1, :]
    c = cos[None, :, None, None].astype(x.dtype)
    s = sin[None, :, None, None].astype(x.dtype)
    out = jnp.stack([x1 * c - x2 * s, x2 * c + x1 * s], axis=-2)
    return out.reshape(x.shape)


def diff_attend(q, k, v, lam):
    s = jnp.einsum('bqhmd,bkhmd->bhmqk', q, k).astype(jnp.float32) * (DQK ** -0.5)
    p = jax.nn.softmax(s, axis=-1)
    a = (p[:, :, 0] - lam * p[:, :, 1]).astype(v.dtype)
    return jnp.einsum('bhqk,bkhe->bqhe', a, v)


def diff_attend_blocked(q, k, v, lam):
    b, n = q.shape[0], q.shape[1]
    nb = n // Q_BLOCK
    qb = q.reshape(b, nb, Q_BLOCK, N_HEADS, 2, DQK).transpose(1, 0, 2, 3, 4, 5)
    ob = lax.map(lambda blk: diff_attend(blk, k, v, lam), qb)
    return ob.transpose(1, 0, 2, 3, 4).reshape(b, n, N_HEADS, DV)


def head_rmsnorm(o, g):
    of = o.astype(jnp.float32)
    y = of * lax.rsqrt(jnp.mean(jnp.square(of), axis=-1, keepdims=True) + HEAD_NORM_EPS)
    return y.astype(o.dtype) * g.reshape(N_HEADS, DV)


def short_conv(gb, gc, gx, w, b):
    g = gc * gx
    gp = jnp.pad(g, ((0, 0), (1, 1), (0, 0)))
    y = w[0] * gp[:, :-2] + w[1] * gp[:, 1:-1] + w[2] * gp[:, 2:] + b
    return gb * y


def split_heads(q, k, v):
    b, n = q.shape[0], q.shape[1]
    return (q.reshape(b, n, N_HEADS, 2, DQK), k.reshape(b, n, N_HEADS, 2, DQK),
            v.reshape(b, n, N_HEADS, DV))


def mixer(u, uc, w_in, lam, lam_init, attn_g, conv_w, conv_b, w_out, cos, sin, ctx_out):
    b, n, _ = u.shape
    q, k, v, gb, gc, gx = jnp.split(u @ w_in, 6, axis=-1)
    q, k, v = split_heads(q, k, v)
    q = apply_rope(q, cos, sin)
    k = apply_rope(k, cos, sin)
    if ctx_out:
        qc, kc, vc, gbc, gcc, gxc = jnp.split(uc @ w_in, 6, axis=-1)
        qc, kc, vc = split_heads(qc, kc, vc)
    else:
        kc, vc = jnp.split(uc @ w_in[:, D_ATT:3 * D_ATT], 2, axis=-1)
        lc = kc.shape[1]
        kc = kc.reshape(b, lc, N_HEADS, 2, DQK)
        vc = vc.reshape(b, lc, N_HEADS, DV)
    k_all = jnp.concatenate([k, kc], axis=1)
    v_all = jnp.concatenate([v, vc], axis=1)
    att = diff_attend_blocked(q, k_all, v_all, lam)
    att = (head_rmsnorm(att, attn_g) * (1.0 - lam_init)).reshape(b, n, D_ATT)
    conv = short_conv(gb, gc, gx, conv_w, conv_b)
    y = jnp.concatenate([att, conv], axis=-1) @ w_out
    if not ctx_out:
        return y, None
    lc = qc.shape[1]
    att_c = diff_attend(qc, kc, vc, lam)
    att_c = (head_rmsnorm(att_c, attn_g) * (1.0 - lam_init)).reshape(b, lc, D_ATT)
    conv_c = short_conv(gbc, gcc, gxc, conv_w, conv_b)
    yc = jnp.concatenate([att_c, conv_c], axis=-1) @ w_out
    return y, yc


def moe(h, router_w, router_bias, w_gate, w_up, w_down):
    shp = h.shape
    t = h.reshape(-1, shp[-1])
    scores = jax.nn.sigmoid((t @ router_w).astype(jnp.float32))
    sel = scores + router_bias.astype(jnp.float32)
    grp = sel.reshape(-1, N_GROUPS, EXPERTS_PER_GROUP)
    gscore = lax.top_k(grp, TOP_K)[0].sum(-1)
    best = jnp.argmax(gscore, axis=-1)
    in_group = (jnp.arange(N_EXPERTS) // EXPERTS_PER_GROUP)[None, :] == best[:, None]
    masked = jnp.where(in_group, sel, -jnp.inf)
    _, idx = lax.top_k(masked, TOP_K)
    w = jnp.take_along_axis(scores, idx, axis=-1)
    w = w / jnp.sum(w, axis=-1, keepdims=True)
    gates = jnp.sum(jax.nn.one_hot(idx, N_EXPERTS, dtype=jnp.float32) * w[..., None], axis=1).astype(t.dtype)
    out = jnp.zeros_like(t)
    for e in range(N_EXPERTS):
        he = jax.nn.silu(t @ w_gate[e]) * (t @ w_up[e])
        out = out + gates[:, e:e + 1] * (he @ w_down[e])
    return out.reshape(shp)


def setup_inputs(seed: int = 0) -> dict:
    key = jax.random.key(seed)
    ks = jax.random.split(key, 24)
    nrm = lambda k, s, sc: jax.random.normal(k, s, jnp.float32) * sc
    D = D_MODEL
    return {
        'x': nrm(ks[0], (BATCH, SEQ, D), 1.0),
        'c': nrm(ks[1], (BATCH, D), 1.0),
        'ctx': nrm(ks[2], (BATCH, CTX_LEN, D), 1.0),
        'c_ctx': nrm(ks[3], (D,), 1.0),
        'w_mod': nrm(ks[4], (DEPTH, D, 6 * D), 0.5 * D ** -0.5),
        'b_mod': nrm(ks[5], (DEPTH, 6 * D), 0.02),
        'w_in': nrm(ks[6], (DEPTH, D, 3 * D_ATT + 3 * D_CONV), D ** -0.5),
        'diff_lambda': nrm(ks[7], (DEPTH, 4, DQK), 0.1),
        'attn_norm_g': 1.0 + nrm(ks[8], (DEPTH, D_ATT), 0.02),
        'conv_w': nrm(ks[9], (DEPTH, 3, D_CONV), 3 ** -0.5),
        'conv_b': nrm(ks[10], (DEPTH, D_CONV), 0.02),
        'w_out': nrm(ks[11], (DEPTH, D_MIX, D), BETA * D_MIX ** -0.5),
        'ln1_g': 1.0 + nrm(ks[12], (DEPTH, D), 0.02),
        'ln1_b': nrm(ks[13], (DEPTH, D), 0.02),
        'ln2_g': 1.0 + nrm(ks[14], (DEPTH, D), 0.02),
        'ln2_b': nrm(ks[15], (DEPTH, D), 0.02),
        'router_w': nrm(ks[16], (D, N_EXPERTS), D ** -0.5),
        'router_bias': nrm(ks[17], (N_EXPERTS,), 0.01),
        'w_gate': nrm(ks[18], (DEPTH, N_EXPERTS, D, D_EXPERT), D ** -0.5),
        'w_up': nrm(ks[19], (DEPTH, N_EXPERTS, D, D_EXPERT), D ** -0.5),
        'w_down': nrm(ks[20], (DEPTH, N_EXPERTS, D_EXPERT, D), BETA * D_EXPERT ** -0.5),
    }


def reference(x, c, ctx, c_ctx, w_mod, b_mod, w_in, diff_lambda, attn_norm_g, conv_w, conv_b, w_out,
              ln1_g, ln1_b, ln2_g, ln2_b, router_w, router_bias, w_gate, w_up, w_down):
    n_lat = x.shape[1]
    cos, sin = rope_tables(n_lat)
    for l in range(DEPTH):
        last = l == DEPTH - 1
        m = (jax.nn.silu(c) @ w_mod[l] + b_mod[l])[:, None, :]
        sh1, sc1, g1, sh2, sc2, g2 = jnp.split(m, 6, axis=-1)
        mc = jax.nn.silu(c_ctx) @ w_mod[l] + b_mod[l]
        sh1c, sc1c, g1c, sh2c, sc2c, g2c = jnp.split(mc, 6, axis=-1)
        lam_init = 0.8 - 0.6 * math.exp(-0.3 * l)
        lp = diff_lambda[l].astype(jnp.float32)
        lam = jnp.exp(jnp.dot(lp[0], lp[1])) - jnp.exp(jnp.dot(lp[2], lp[3])) + lam_init
        u = modulate(x, sh1, sc1)
        uc = modulate(ctx, sh1c, sc1c)
        mix_x, mix_c = mixer(u, uc, w_in[l], lam, lam_init, attn_norm_g[l], conv_w[l], conv_b[l], w_out[l],
                             cos, sin, not last)
        x = layer_norm(ALPHA * x + g1 * mix_x, ln1_g[l], ln1_b[l])
        if last:
            f = moe(modulate(x, sh2, sc2), router_w, router_bias, w_gate[l], w_up[l], w_down[l])
            x = layer_norm(ALPHA * x + g2 * f, ln2_g[l], ln2_b[l])
        else:
            ctx = layer_norm(ALPHA * ctx + g1c * mix_c, ln1_g[l], ln1_b[l])
            lc = ctx.shape[1]
            h = jnp.concatenate([modulate(ctx, sh2c, sc2c), modulate(x, sh2, sc2)], axis=1)
            f = moe(h, router_w, router_bias, w_gate[l], w_up[l], w_down[l])
            ctx = layer_norm(ALPHA * ctx + g2c * f[:, :lc], ln2_g[l], ln2_b[l])
            x = layer_norm(ALPHA * x + g2 * f[:, lc:], ln2_g[l], ln2_b[l])
    return x
```

```python
import functools
import math

import jax
import jax.numpy as jnp
from jax import lax
from jax.experimental import pallas as pl
from jax.experimental.pallas import tpu as pltpu

D_MODEL = 1024
BATCH = 16
SEQ = 4096
DEPTH = 2
GRID_W = 64
CTX_LEN = 256
D_ATT = D_MODEL // 2
D_CONV = D_MODEL - D_ATT
N_HEADS = 4
DV = D_ATT // N_HEADS
DQK = DV // 2
N_FREQ = DQK // 4
ROPE_THETA = 10000.0
N_EXPERTS = 16
N_GROUPS = 4
EXPERTS_PER_GROUP = N_EXPERTS // N_GROUPS
D_EXPERT = 512
LN_EPS = 1e-6
HEAD_NORM_EPS = 1e-5
ALPHA = (2 * DEPTH) ** 0.25

F32 = jnp.float32
BF16 = jnp.bfloat16

LANES = 128
SUBLANES = 8
VMEM_LIMIT_BYTES = 48 * 1024 * 1024

N_COND = BATCH + 1
N_COND_PAD = 24
CTX_ROW = BATCH
N_MOD = 6
N_MOD_PAD = SUBLANES

TM_LAT = 512
TM_CTX = CTX_LEN
TQ = 256
RB = 16
TM_MOE = 1024


def _params(*sem):
    return pltpu.CompilerParams(dimension_semantics=sem, vmem_limit_bytes=VMEM_LIMIT_BYTES)


def _ln(x):
    mu = jnp.mean(x, axis=-1, keepdims=True)
    xc = x - mu
    var = jnp.mean(xc * xc, axis=-1, keepdims=True)
    return xc * lax.rsqrt(var + LN_EPS)


def _sigmoid(x):
    return 1.0 / (1.0 + jnp.exp(-x))


def _split_bf16(x):
    hi = x.astype(BF16)
    lo = (x - hi.astype(F32)).astype(BF16)
    return hi, lo


def _mod_kernel(cc_ref, w_ref, b_ref, o_ref):
    a = cc_ref[...]
    a = a * _sigmoid(a)
    a_hi, a_lo = _split_bf16(a)
    w_hi, w_lo = _split_bf16(w_ref[...])
    m = (jnp.dot(a_hi, w_hi, preferred_element_type=F32)
         + jnp.dot(a_lo, w_hi, preferred_element_type=F32)
         + jnp.dot(a_hi, w_lo, preferred_element_type=F32))
    o_ref[...] = m + b_ref[...]


def _modulation(cc, w_mod, b_mod):
    d = D_MODEL
    out = pl.pallas_call(
        _mod_kernel,
        grid=(DEPTH, N_MOD),
        in_specs=[
            pl.BlockSpec((N_COND_PAD, d), lambda l, j: (0, 0)),
            pl.BlockSpec((None, d, d), lambda l, j: (l, 0, j)),
            pl.BlockSpec((None, 1, d), lambda l, j: (l, 0, j)),
        ],
        out_specs=pl.BlockSpec((None, N_COND_PAD, d), lambda l, j: (l, 0, j)),
        out_shape=jax.ShapeDtypeStruct((DEPTH, N_COND_PAD, N_MOD * d), F32),
        compiler_params=_params("parallel", "parallel"),
        name="modulation",
    )(cc, w_mod, b_mod.reshape(DEPTH, 1, N_MOD * d))
    m = out.reshape(DEPTH, N_COND_PAD, N_MOD, d)
    return jnp.pad(m, ((0, 0), (0, 0), (0, N_MOD_PAD - N_MOD), (0, 0)))


def _inproj_kernel(*refs, use_rope, kv_only):
    x_ref, m_ref, w_ref = refs[:3]
    rest = refs[3:]
    if use_rope:
        ra_ref, rb_ref, rc_ref = rest[:3]
        rest = rest[3:]
    u = _ln(x_ref[...]) * (1.0 + m_ref[1:2, :]) + m_ref[0:1, :]
    y = jnp.dot(u.astype(BF16), w_ref[...], preferred_element_type=F32)
    if kv_only:
        kt_ref, v_ref = rest
        kt_ref[...] = y[:, :D_ATT].T.astype(BF16)
        v_ref[...] = y[:, D_ATT:2 * D_ATT].astype(BF16)
        return
    q_ref, kt_ref, v_ref, g_ref, gb_ref = rest
    q = y[:, :D_ATT] * (DQK ** -0.5)
    k = y[:, D_ATT:2 * D_ATT]
    if use_rope:
        ra, rb, rc = ra_ref[...], rb_ref[...], rc_ref[...]

        def rope(t):
            cols = []
            for j in range(D_ATT // LANES):
                tj = t[:, j * LANES:(j + 1) * LANES]
                cols.append(tj * ra
                            + pltpu.roll(tj, LANES - N_FREQ, 1) * rb
                            + pltpu.roll(tj, N_FREQ, 1) * rc)
            return jnp.concatenate(cols, axis=1)

        q = rope(q)
        k = rope(k)
    q_ref[...] = q.astype(BF16)
    kt_ref[...] = k.T.astype(BF16)
    v_ref[...] = y[:, 2 * D_ATT:3 * D_ATT].astype(BF16)
    gb_ref[...] = y[:, 3 * D_ATT:3 * D_ATT + D_CONV]
    g_ref[...] = y[:, 3 * D_ATT + D_CONV:3 * D_ATT + 2 * D_CONV] * y[:, 3 * D_ATT + 2 * D_CONV:]


def _inproj(x, mod_l, w, rope_tabs, *, tm, cond_row, kv_only=False):
    b, n, d = x.shape
    use_rope = rope_tabs is not None
    wn = w.shape[1]
    in_specs = [
        pl.BlockSpec((None, tm, d), lambda bi, i: (bi, i, 0)),
        pl.BlockSpec((None, N_MOD_PAD, d), lambda bi, i: (cond_row(bi), 0, 0)),
        pl.BlockSpec((d, wn), lambda bi, i: (0, 0)),
    ]
    args = [x, mod_l, w]
    if use_rope:
        in_specs += [pl.BlockSpec((tm, LANES), lambda bi, i: (i, 0))] * 3
        args += list(rope_tabs)
    kt_spec = pl.BlockSpec((None, D_ATT, tm), lambda bi, i: (bi, 0, i))
    row_spec = pl.BlockSpec((None, tm, D_ATT), lambda bi, i: (bi, i, 0))
    kt_shape = jax.ShapeDtypeStruct((b, D_ATT, n), BF16)
    if kv_only:
        out_specs = [kt_spec, row_spec]
        out_shape = [kt_shape, jax.ShapeDtypeStruct((b, n, D_ATT), BF16)]
    else:
        out_specs = [row_spec, kt_spec, row_spec, row_spec, row_spec]
        out_shape = [jax.ShapeDtypeStruct((b, n, D_ATT), BF16), kt_shape,
                     jax.ShapeDtypeStruct((b, n, D_ATT), BF16),
                     jax.ShapeDtypeStruct((b, n, D_CONV), F32),
                     jax.ShapeDtypeStruct((b, n, D_CONV), F32)]
    return pl.pallas_call(
        functools.partial(_inproj_kernel, use_rope=use_rope, kv_only=kv_only),
        grid=(b, n // tm),
        in_specs=in_specs,
        out_specs=out_specs,
        out_shape=out_shape,
        compiler_params=_params("parallel", "parallel"),
        name="inproj",
    )(*args)


def _attn_kernel(*refs, n_seg, lam_init):
    lp_ref, q_ref = refs[:2]
    seg_refs = refs[2:2 + 2 * n_seg]
    gn_ref, o_ref, s_ref, a_ref, r_ref = refs[2 + 2 * n_seg:]
    kt_refs = seg_refs[0::2]
    v_refs = seg_refs[1::2]
    tq = q_ref.shape[0]

    lp = lp_ref[...]
    lam = (jnp.exp(jnp.sum(lp[0:1, :] * lp[1:2, :], axis=-1, keepdims=True))
           - jnp.exp(jnp.sum(lp[2:3, :] * lp[3:4, :], axis=-1, keepdims=True)) + lam_init)

    q = q_ref[...]
    lane = lax.broadcasted_iota(jnp.int32, q.shape, 1)
    zero = jnp.zeros_like(q)
    qz = jnp.concatenate([jnp.where(lane < DQK, q, zero), jnp.where(lane >= DQK, q, zero)], axis=0)
    off = 0
    for kt_ref in kt_refs:
        nk = kt_ref.shape[1]
        s_ref[:, off:off + nk] = jnp.dot(qz, kt_ref[...], preferred_element_type=F32)
        off += nk

    def block(r, carry):
        r0 = pl.multiple_of(r * RB, RB)
        x1 = s_ref[pl.ds(r0, RB), :]
        e1 = jnp.exp(x1 - jnp.max(x1, axis=-1, keepdims=True))
        l1 = jnp.sum(e1, axis=-1, keepdims=True)
        x2 = s_ref[pl.ds(tq + r0, RB), :]
        e2 = jnp.exp(x2 - jnp.max(x2, axis=-1, keepdims=True))
        l2 = jnp.sum(e2, axis=-1, keepdims=True)
        a_ref[pl.ds(r0, RB), :] = (e1 - (lam * l1 / l2) * e2).astype(BF16)
        r_ref[pl.ds(r0, RB), :] = jnp.broadcast_to(1.0 / l1, (RB, LANES))
        return carry

    lax.fori_loop(0, tq // RB, block, 0)

    o = None
    off = 0
    for v_ref in v_refs:
        nk = v_ref.shape[0]
        part = jnp.dot(a_ref[:, off:off + nk], v_ref[...], preferred_element_type=F32)
        o = part if o is None else o + part
        off += nk
    o = o * r_ref[...]
    y = o * lax.rsqrt(jnp.mean(o * o, axis=-1, keepdims=True) + HEAD_NORM_EPS)
    o_ref[...] = (y * gn_ref[...] * (1.0 - lam_init)).astype(BF16)


def _attention(lam_params, q, segs, attn_g, *, lam_init):
    b, n, _ = q.shape
    tq = min(TQ, n)
    nk_total = sum(kt.shape[2] for kt, _ in segs)
    in_specs = [
        pl.BlockSpec((4, DQK), lambda bi, h, i: (0, 0)),
        pl.BlockSpec((None, tq, DV), lambda bi, h, i: (bi, i, h)),
    ]
    args = [lam_params, q]
    for kt, v in segs:
        nk = kt.shape[2]
        in_specs += [pl.BlockSpec((None, DV, nk), lambda bi, h, i: (bi, h, 0)),
                     pl.BlockSpec((None, nk, DV), lambda bi, h, i: (bi, 0, h))]
        args += [kt, v]
    in_specs.append(pl.BlockSpec((1, DV), lambda bi, h, i: (0, h)))
    args.append(attn_g.reshape(1, D_ATT))
    return pl.pallas_call(
        functools.partial(_attn_kernel, n_seg=len(segs), lam_init=lam_init),
        grid=(b, N_HEADS, n // tq),
        in_specs=in_specs,
        out_specs=pl.BlockSpec((None, tq, DV), lambda bi, h, i: (bi, i, h)),
        out_shape=jax.ShapeDtypeStruct((b, n, D_ATT), BF16),
        scratch_shapes=[pltpu.VMEM((2 * tq, nk_total), F32),
                        pltpu.VMEM((tq, nk_total), BF16),
                        pltpu.VMEM((tq, LANES), F32)],
        compiler_params=_params("parallel", "parallel", "arbitrary"),
        name="diff_attention",
    )(*args)


def _top2_sum(a, b, c, d):
    hi1, lo1 = jnp.maximum(a, b), jnp.minimum(a, b)
    hi2, lo2 = jnp.maximum(c, d), jnp.minimum(c, d)
    return jnp.maximum(hi1, hi2) + jnp.maximum(jnp.minimum(hi1, hi2), jnp.maximum(lo1, lo2))


def _router_gates(h, rwt_ref, rbias_ref):
    tm = h.shape[0]
    h_hi, h_lo = _split_bf16(h)
    w_hi, w_lo = _split_bf16(rwt_ref[...])
    nt = (((1,), (1,)), ((), ()))
    logits = (lax.dot_general(w_hi, h_hi, nt, preferred_element_type=F32)
              + lax.dot_general(w_hi, h_lo, nt, preferred_element_type=F32)
              + lax.dot_general(w_lo, h_hi, nt, preferred_element_type=F32))
    scores = _sigmoid(logits)
    sel = scores + rbias_ref[...]
    srow = [sel[e:e + 1, :] for e in range(N_EXPERTS)]
    crow = [scores[e:e + 1, :] for e in range(N_EXPERTS)]
    gscore = [_top2_sum(*srow[EXPERTS_PER_GROUP * g:EXPERTS_PER_GROUP * (g + 1)]) for g in range(N_GROUPS)]
    best = jnp.zeros((1, tm), jnp.int32)
    best_v = gscore[0]
    for g in range(1, N_GROUPS):
        upd = gscore[g] > best_v
        best = jnp.where(upd, g, best)
        best_v = jnp.where(upd, gscore[g], best_v)
    cand, cand_score = [], []
    for j in range(EXPERTS_PER_GROUP):
        cs, cc = srow[j], crow[j]
        for g in range(1, N_GROUPS):
            cs = jnp.where(best == g, srow[EXPERTS_PER_GROUP * g + j], cs)
            cc = jnp.where(best == g, crow[EXPERTS_PER_GROUP * g + j], cc)
        cand.append(cs)
        cand_score.append(cc)
    i0 = jnp.zeros((1, tm), jnp.int32)
    v0, w0 = cand[0], cand_score[0]
    for j in range(1, EXPERTS_PER_GROUP):
        upd = cand[j] > v0
        i0 = jnp.where(upd, j, i0)
        v0 = jnp.where(upd, cand[j], v0)
        w0 = jnp.where(upd, cand_score[j], w0)
    i1 = jnp.zeros((1, tm), jnp.int32)
    v1 = jnp.full((1, tm), -jnp.inf, F32)
    w1 = jnp.zeros((1, tm), F32)
    for j in range(EXPERTS_PER_GROUP):
        upd = jnp.logical_and(i0 != j, cand[j] > v1)
        i1 = jnp.where(upd, j, i1)
        v1 = jnp.where(upd, cand[j], v1)
        w1 = jnp.where(upd, cand_score[j], w1)
    wsum = w0 + w1
    e0 = best * EXPERTS_PER_GROUP + i0
    e1 = best * EXPERTS_PER_GROUP + i1
    eid = lax.broadcasted_iota(jnp.int32, (N_EXPERTS, tm), 0)
    gates_t = jnp.where(eid == e0, w0 / wsum, 0.0) + jnp.where(eid == e1, w1 / wsum, 0.0)
    gates_t = jnp.concatenate([gates_t, jnp.zeros((LANES - N_EXPERTS, tm), F32)], axis=0)
    return gates_t.T


def _outproj_kernel(att_ref, g_ref, gprev_ref, gnext_ref, gb_ref, x_ref, m_ref, wo_ref, cw_ref, cb_ref,
                    lng_ref, lnb_ref, rwt_ref, rbias_ref, x1_ref, h_ref, gates_ref, *, n_tiles):
    i = pl.program_id(1)
    g = g_ref[...]
    tm = g.shape[0]
    rows = lax.broadcasted_iota(jnp.int32, g.shape, 0)
    prev_row = jnp.where(i > 0, gprev_ref[SUBLANES - 1:SUBLANES, :], 0.0)
    next_row = jnp.where(i < n_tiles - 1, gnext_ref[0:1, :], 0.0)
    g_m1 = jnp.where(rows == 0, prev_row, pltpu.roll(g, 1, 0))
    g_p1 = jnp.where(rows == tm - 1, next_row, pltpu.roll(g, tm - 1, 0))
    conv = cw_ref[0:1, :] * g_m1 + cw_ref[1:2, :] * g + cw_ref[2:3, :] * g_p1 + cb_ref[...]
    yc = (gb_ref[...] * conv).astype(BF16)
    y = (jnp.dot(att_ref[...], wo_ref[:D_ATT, :], preferred_element_type=F32)
         + jnp.dot(yc, wo_ref[D_ATT:, :], preferred_element_type=F32))
    x1 = _ln(ALPHA * x_ref[...] + m_ref[2:3, :] * y) * lng_ref[...] + lnb_ref[...]
    x1_ref[...] = x1
    h = _ln(x1) * (1.0 + m_ref[4:5, :]) + m_ref[3:4, :]
    h_ref[...] = h.astype(BF16)
    gates_ref[...] = _router_gates(h, rwt_ref, rbias_ref)


def _outproj(att, g, gb, x, mod_l, wo, cw, cb, lng, lnb, rwt, rbias, *, tm, cond_row):
    b, n, d = x.shape
    n_tiles = n // tm
    g8 = g.reshape(b, n // SUBLANES, SUBLANES, D_CONV)
    tpb = tm // SUBLANES
    row = lambda w: pl.BlockSpec((None, tm, w), lambda bi, i: (bi, i, 0))
    full = lambda s: pl.BlockSpec(s, lambda bi, i: (0,) * len(s))
    return pl.pallas_call(
        functools.partial(_outproj_kernel, n_tiles=n_tiles),
        grid=(b, n_tiles),
        in_specs=[
            row(D_ATT), row(D_CONV),
            pl.BlockSpec((None, None, SUBLANES, D_CONV), lambda bi, i: (bi, jnp.maximum(i * tpb - 1, 0), 0, 0)),
            pl.BlockSpec((None, None, SUBLANES, D_CONV),
                         lambda bi, i: (bi, jnp.minimum((i + 1) * tpb, n // SUBLANES - 1), 0, 0)),
            row(D_CONV), row(d),
            pl.BlockSpec((None, N_MOD_PAD, d), lambda bi, i: (cond_row(bi), 0, 0)),
            full((d, d)), full((3, D_CONV)), full((1, D_CONV)), full((1, d)), full((1, d)),
            full((N_EXPERTS, d)), full((N_EXPERTS, 1)),
        ],
        out_specs=[row(d), row(d), row(LANES)],
        out_shape=[jax.ShapeDtypeStruct((b, n, d), F32), jax.ShapeDtypeStruct((b, n, d), BF16),
                   jax.ShapeDtypeStruct((b, n, LANES), F32)],
        compiler_params=_params("parallel", "parallel"),
        name="outproj_ln1_router",
    )(att, g, g8, g8, gb, x, mod_l, wo, cw, cb, lng, lnb, rwt, rbias)


def _moe_dense_kernel(h_ref, gates_ref, wg_ref, wu_ref, wd_ref, x1_ref, m_ref, lng_ref, lnb_ref, o_ref, acc_ref):
    e = pl.program_id(1)

    @pl.when(e == 0)
    def _():
        acc_ref[...] = jnp.zeros_like(acc_ref)

    h = h_ref[...]
    gate = jnp.dot(h, wg_ref[...], preferred_element_type=F32)
    up = jnp.dot(h, wu_ref[...], preferred_element_type=F32)
    he = (gate * _sigmoid(gate) * up).astype(BF16)
    y = jnp.dot(he, wd_ref[...], preferred_element_type=F32)
    gts = gates_ref[...]
    lane = lax.broadcasted_iota(jnp.int32, gts.shape, 1)
    gcol = jnp.sum(jnp.where(lane == e, gts, 0.0), axis=1, keepdims=True)
    acc_ref[...] += gcol * y

    @pl.when(e == N_EXPERTS - 1)
    def _():
        o_ref[...] = _ln(ALPHA * x1_ref[...] + m_ref[5:6, :] * acc_ref[...]) * lng_ref[...] + lnb_ref[...]


def _moe_dense(h, gates, wg, wu, wd, x1, mod_l, lng, lnb, *, cond_row):
    n, d = h.shape
    tm = TM_MOE
    tok = lambda w: pl.BlockSpec((tm, w), lambda i, e: (i, 0))
    vec = pl.BlockSpec((1, d), lambda i, e: (0, 0))
    return pl.pallas_call(
        _moe_dense_kernel,
        grid=(n // tm, N_EXPERTS),
        in_specs=[
            tok(d), tok(LANES),
            pl.BlockSpec((None, d, D_EXPERT), lambda i, e: (e, 0, 0)),
            pl.BlockSpec((None, d, D_EXPERT), lambda i, e: (e, 0, 0)),
            pl.BlockSpec((None, D_EXPERT, d), lambda i, e: (e, 0, 0)),
            tok(d),
            pl.BlockSpec((None, N_MOD_PAD, d), lambda i, e: (cond_row(i), 0, 0)),
            vec, vec,
        ],
        out_specs=tok(d),
        out_shape=jax.ShapeDtypeStruct((n, d), F32),
        scratch_shapes=[pltpu.VMEM((tm, d), F32)],
        compiler_params=_params("parallel", "arbitrary"),
        name="moe_dense_ln2",
    )(h, gates, wg, wu, wd, x1, mod_l, lng, lnb)


def _rope_tables():
    t = jnp.arange(SEQ, dtype=jnp.int32)
    pos = jnp.stack([(t // GRID_W).astype(F32), (t % GRID_W).astype(F32)], axis=1)
    inv = 1.0 / (ROPE_THETA ** (jnp.arange(N_FREQ, dtype=F32) / N_FREQ))
    ang = pos[:, :, None] * inv
    cos, sin = jnp.cos(ang), jnp.sin(ang)
    zero = jnp.zeros_like(sin)
    expand = lambda first, second: jnp.tile(
        jnp.stack([first, second], axis=2).reshape(SEQ, DQK), (1, LANES // DQK))
    return expand(cos, cos), expand(-sin, zero), expand(zero, sin)


def kernel(x, c, ctx, c_ctx, w_mod, b_mod, w_in, diff_lambda, attn_norm_g, conv_w, conv_b, w_out,
           ln1_g, ln1_b, ln2_g, ln2_b, router_w, router_bias, w_gate, w_up, w_down):
    assert x.shape == (BATCH, SEQ, D_MODEL) and ctx.shape == (BATCH, CTX_LEN, D_MODEL)
    d = D_MODEL
    cc = jnp.concatenate([c, c_ctx[None, :], jnp.zeros((N_COND_PAD - N_COND, d), F32)], axis=0)
    mod = _modulation(cc, w_mod, b_mod)
    rope_tabs = _rope_tables()
    w_in_b, w_out_b = w_in.astype(BF16), w_out.astype(BF16)
    wg_b, wu_b, wd_b = w_gate.astype(BF16), w_up.astype(BF16), w_down.astype(BF16)
    rwt = router_w.T
    rbias = router_bias.reshape(N_EXPERTS, 1)

    lat_row = lambda bi: bi
    ctx_row = lambda bi: CTX_ROW
    lat_tile_row = lambda i: i // (SEQ // TM_MOE)

    for l in range(DEPTH):
        last = l == DEPTH - 1
        lam_init = 0.8 - 0.6 * math.exp(-0.3 * l)
        vec = lambda p, w: p[l].reshape(1, w)
        post = (w_out_b[l], conv_w[l], vec(conv_b, D_CONV), vec(ln1_g, d), vec(ln1_b, d), rwt, rbias)

        q, kt, v, g, gb = _inproj(x, mod[l], w_in_b[l], rope_tabs, tm=TM_LAT, cond_row=lat_row)
        if last:
            ktc, vc = _inproj(ctx, mod[l], w_in_b[l][:, D_ATT:3 * D_ATT], None, tm=TM_CTX, cond_row=ctx_row,
                              kv_only=True)
        else:
            qc, ktc, vc, gc, gbc = _inproj(ctx, mod[l], w_in_b[l], None, tm=TM_CTX, cond_row=ctx_row)

        att = _attention(diff_lambda[l], q, [(kt, v), (ktc, vc)], attn_norm_g[l], lam_init=lam_init)
        x1, h, gates = _outproj(att, g, gb, x, mod[l], *post, tm=TM_LAT, cond_row=lat_row)
        moe_w = (wg_b[l], wu_b[l], wd_b[l])
        x = _moe_dense(h.reshape(-1, d), gates.reshape(-1, LANES), *moe_w, x1.reshape(-1, d), mod[l],
                       vec(ln2_g, d), vec(ln2_b, d), cond_row=lat_tile_row).reshape(BATCH, SEQ, d)
        if not last:
            att_c = _attention(diff_lambda[l], qc, [(ktc, vc)], attn_norm_g[l], lam_init=lam_init)
            c1, hc, gates_c = _outproj(att_c, gc, gbc, ctx, mod[l], *post, tm=TM_CTX, cond_row=ctx_row)
            ctx = _moe_dense(hc.reshape(-1, d), gates_c.reshape(-1, LANES), *moe_w, c1.reshape(-1, d), mod[l],
                             vec(ln2_g, d), vec(ln2_b, d), cond_row=lambda i: CTX_ROW
                             ).reshape(BATCH, CTX_LEN, d)
    return x
```

```python
import functools
import math

import jax
import jax.numpy as jnp
from jax import lax
from jax.experimental import pallas as pl
from jax.experimental.pallas import tpu as pltpu

D_MODEL = 1024
BATCH = 16
SEQ = 4096
DEPTH = 2
GRID_W = 64
CTX_LEN = 256
D_ATT = D_MODEL // 2
D_CONV = D_MODEL - D_ATT
N_HEADS = 4
DV = D_ATT // N_HEADS
DQK = DV // 2
N_FREQ = DQK // 4
ROPE_THETA = 10000.0
N_EXPERTS = 16
N_GROUPS = 4
EXPERTS_PER_GROUP = N_EXPERTS // N_GROUPS
D_EXPERT = 512
LN_EPS = 1e-6
HEAD_NORM_EPS = 1e-5
ALPHA = (2 * DEPTH) ** 0.25

F32 = jnp.float32
BF16 = jnp.bfloat16

LANES = 128
SUBLANES = 8
VMEM_LIMIT_BYTES = 48 * 1024 * 1024

N_COND = BATCH + 1
N_COND_PAD = 24
CTX_ROW = BATCH
N_MOD = 6
N_MOD_PAD = SUBLANES

TM_LAT = 512
TM_CTX = CTX_LEN
TQ = 256
N_SUB = 2
RB = 16
LOG2E = math.log2(math.e)
TM_MOE = 1024


def _params(*sem):
    return pltpu.CompilerParams(dimension_semantics=sem, vmem_limit_bytes=VMEM_LIMIT_BYTES)


def _ln(x):
    mu = jnp.mean(x, axis=-1, keepdims=True)
    xc = x - mu
    var = jnp.mean(xc * xc, axis=-1, keepdims=True)
    return xc * lax.rsqrt(var + LN_EPS)


def _sigmoid(x):
    return 1.0 / (1.0 + jnp.exp(-x))


def _split_bf16(x):
    hi = x.astype(BF16)
    lo = (x - hi.astype(F32)).astype(BF16)
    return hi, lo


def _mod_kernel(cc_ref, w_ref, b_ref, o_ref):
    a = cc_ref[...]
    a = a * _sigmoid(a)
    a_hi, a_lo = _split_bf16(a)
    w_hi, w_lo = _split_bf16(w_ref[...])
    m = (jnp.dot(a_hi, w_hi, preferred_element_type=F32)
         + jnp.dot(a_lo, w_hi, preferred_element_type=F32)
         + jnp.dot(a_hi, w_lo, preferred_element_type=F32))
    o_ref[...] = m + b_ref[...]


def _modulation(cc, w_mod, b_mod):
    d = D_MODEL
    out = pl.pallas_call(
        _mod_kernel,
        grid=(DEPTH, N_MOD),
        in_specs=[
            pl.BlockSpec((N_COND_PAD, d), lambda l, j: (0, 0)),
            pl.BlockSpec((None, d, d), lambda l, j: (l, 0, j)),
            pl.BlockSpec((None, 1, d), lambda l, j: (l, 0, j)),
        ],
        out_specs=pl.BlockSpec((None, N_COND_PAD, d), lambda l, j: (l, 0, j)),
        out_shape=jax.ShapeDtypeStruct((DEPTH, N_COND_PAD, N_MOD * d), F32),
        compiler_params=_params("parallel", "parallel"),
        name="modulation",
    )(cc, w_mod, b_mod.reshape(DEPTH, 1, N_MOD * d))
    m = out.reshape(DEPTH, N_COND_PAD, N_MOD, d)
    return jnp.pad(m, ((0, 0), (0, 0), (0, N_MOD_PAD - N_MOD), (0, 0)))


def _inproj_kernel(*refs, use_rope, kv_only):
    x_ref, m_ref, w_ref = refs[:3]
    rest = refs[3:]
    if use_rope:
        ra_ref, rb_ref, rc_ref = rest[:3]
        rest = rest[3:]
    u = _ln(x_ref[...]) * (1.0 + m_ref[1:2, :]) + m_ref[0:1, :]
    y = jnp.dot(u.astype(BF16), w_ref[...], preferred_element_type=F32)
    if kv_only:
        kt_ref, v_ref = rest
        kt_ref[...] = y[:, :D_ATT].T.astype(BF16)
        v_ref[...] = y[:, D_ATT:2 * D_ATT].astype(BF16)
        return
    q_ref, kt_ref, v_ref, g_ref, gb_ref = rest
    q = y[:, :D_ATT] * (DQK ** -0.5 * LOG2E)
    k = y[:, D_ATT:2 * D_ATT]
    if use_rope:
        ra, rb, rc = ra_ref[...], rb_ref[...], rc_ref[...]

        def rope(t):
            cols = []
            for j in range(D_ATT // LANES):
                tj = t[:, j * LANES:(j + 1) * LANES]
                cols.append(tj * ra
                            + pltpu.roll(tj, LANES - N_FREQ, 1) * rb
                            + pltpu.roll(tj, N_FREQ, 1) * rc)
            return jnp.concatenate(cols, axis=1)

        q = rope(q)
        k = rope(k)
    q_ref[...] = q.astype(BF16)
    kt_ref[...] = k.T.astype(BF16)
    v_ref[...] = y[:, 2 * D_ATT:3 * D_ATT].astype(BF16)
    gb_ref[...] = y[:, 3 * D_ATT:3 * D_ATT + D_CONV]
    g_ref[...] = y[:, 3 * D_ATT + D_CONV:3 * D_ATT + 2 * D_CONV] * y[:, 3 * D_ATT + 2 * D_CONV:]


def _inproj(x, mod_l, w, rope_tabs, *, tm, cond_row, kv_only=False):
    b, n, d = x.shape
    use_rope = rope_tabs is not None
    wn = w.shape[1]
    in_specs = [
        pl.BlockSpec((None, tm, d), lambda bi, i: (bi, i, 0)),
        pl.BlockSpec((None, N_MOD_PAD, d), lambda bi, i: (cond_row(bi), 0, 0)),
        pl.BlockSpec((d, wn), lambda bi, i: (0, 0)),
    ]
    args = [x, mod_l, w]
    if use_rope:
        in_specs += [pl.BlockSpec((tm, LANES), lambda bi, i: (i, 0))] * 3
        args += list(rope_tabs)
    kt_spec = pl.BlockSpec((None, D_ATT, tm), lambda bi, i: (bi, 0, i))
    row_spec = pl.BlockSpec((None, tm, D_ATT), lambda bi, i: (bi, i, 0))
    kt_shape = jax.ShapeDtypeStruct((b, D_ATT, n), BF16)
    if kv_only:
        out_specs = [kt_spec, row_spec]
        out_shape = [kt_shape, jax.ShapeDtypeStruct((b, n, D_ATT), BF16)]
    else:
        out_specs = [row_spec, kt_spec, row_spec, row_spec, row_spec]
        out_shape = [jax.ShapeDtypeStruct((b, n, D_ATT), BF16), kt_shape,
                     jax.ShapeDtypeStruct((b, n, D_ATT), BF16),
                     jax.ShapeDtypeStruct((b, n, D_CONV), F32),
                     jax.ShapeDtypeStruct((b, n, D_CONV), F32)]
    return pl.pallas_call(
        functools.partial(_inproj_kernel, use_rope=use_rope, kv_only=kv_only),
        grid=(b, n // tm),
        in_specs=in_specs,
        out_specs=out_specs,
        out_shape=out_shape,
        compiler_params=_params("parallel", "parallel"),
        name="inproj",
    )(*args)


def _attn_kernel(*refs, n_seg, lam_init):
    lp_ref, q_ref = refs[:2]
    seg_refs = refs[2:2 + 2 * n_seg]
    gn_ref, o_ref, s_ref, a_ref, ml_ref, r_ref = refs[2 + 2 * n_seg:]
    kt_refs = seg_refs[0::2]
    v_refs = seg_refs[1::2]
    n_sub, two_tq, _ = s_ref.shape
    tq = two_tq // 2

    lp = lp_ref[...]
    lam = (jnp.exp(jnp.sum(lp[0:1, :] * lp[1:2, :], axis=-1, keepdims=True))
           - jnp.exp(jnp.sum(lp[2:3, :] * lp[3:4, :], axis=-1, keepdims=True)) + lam_init)

    def scores(j):
        q = q_ref[j * tq:(j + 1) * tq, :]
        lane = lax.broadcasted_iota(jnp.int32, q.shape, 1)
        zero = jnp.zeros_like(q)
        qz = jnp.concatenate([jnp.where(lane < DQK, q, zero), jnp.where(lane >= DQK, q, zero)], axis=0)
        off = 0
        for kt_ref in kt_refs:
            nk = kt_ref.shape[1]
            s_ref[j, :, off:off + nk] = jnp.dot(qz, kt_ref[...], preferred_element_type=F32)
            off += nk

    nk_total = s_ref.shape[2]
    n_rep = nk_total // LANES

    def softmax(j):
        for g in range(two_tq // SUBLANES):
            rows = slice(g * SUBLANES, (g + 1) * SUBLANES)
            m = jnp.max(s_ref[j, rows, :], axis=-1, keepdims=True)
            ml_ref[j, rows, :] = jnp.broadcast_to(m, (SUBLANES, LANES))
        for g in range(two_tq // SUBLANES):
            rows = slice(g * SUBLANES, (g + 1) * SUBLANES)
            e = jnp.exp2(s_ref[j, rows, :] - pltpu.repeat(ml_ref[j, rows, :], n_rep, axis=1))
            s_ref[j, rows, :] = e
            ml_ref[j, rows, :] = jnp.broadcast_to(jnp.sum(e, axis=-1, keepdims=True), (SUBLANES, LANES))
        for r in range(tq // RB):
            rows1 = slice(r * RB, (r + 1) * RB)
            rows2 = slice(tq + r * RB, tq + (r + 1) * RB)
            l1 = ml_ref[j, rows1, :]
            c = pltpu.repeat(lam * l1 / ml_ref[j, rows2, :], n_rep, axis=1)
            a_ref[j, rows1, :] = (s_ref[j, rows1, :] - c * s_ref[j, rows2, :]).astype(BF16)
            r_ref[j, rows1, :] = 1.0 / l1

    def values(j):
        o = None
        off = 0
        for v_ref in v_refs:
            nk = v_ref.shape[0]
            part = jnp.dot(a_ref[j, :, off:off + nk], v_ref[...], preferred_element_type=F32)
            o = part if o is None else o + part
            off += nk
        o = o * r_ref[j]
        y = o * lax.rsqrt(jnp.mean(o * o, axis=-1, keepdims=True) + HEAD_NORM_EPS)
        o_ref[j * tq:(j + 1) * tq, :] = (y * gn_ref[...] * (1.0 - lam_init)).astype(BF16)

    for j in range(n_sub):
        scores(j)
    for j in range(n_sub):
        softmax(j)
        values(j)


def _attention(lam_params, q, segs, attn_g, *, lam_init):
    b, n, _ = q.shape
    tq = min(TQ, n)
    n_sub = min(N_SUB, n // tq)
    tstep = n_sub * tq
    nk_total = sum(kt.shape[2] for kt, _ in segs)
    in_specs = [
        pl.BlockSpec((4, DQK), lambda bi, h, i: (0, 0)),
        pl.BlockSpec((None, tstep, DV), lambda bi, h, i: (bi, i, h)),
    ]
    args = [lam_params, q]
    for kt, v in segs:
        nk = kt.shape[2]
        in_specs += [pl.BlockSpec((None, DV, nk), lambda bi, h, i: (bi, h, 0)),
                     pl.BlockSpec((None, nk, DV), lambda bi, h, i: (bi, 0, h))]
        args += [kt, v]
    in_specs.append(pl.BlockSpec((1, DV), lambda bi, h, i: (0, h)))
    args.append(attn_g.reshape(1, D_ATT))
    return pl.pallas_call(
        functools.partial(_attn_kernel, n_seg=len(segs), lam_init=lam_init),
        grid=(b, N_HEADS, n // tstep),
        in_specs=in_specs,
        out_specs=pl.BlockSpec((None, tstep, DV), lambda bi, h, i: (bi, i, h)),
        out_shape=jax.ShapeDtypeStruct((b, n, D_ATT), BF16),
        scratch_shapes=[pltpu.VMEM((n_sub, 2 * tq, nk_total), F32),
                        pltpu.VMEM((n_sub, tq, nk_total), BF16),
                        pltpu.VMEM((n_sub, 2 * tq, LANES), F32),
                        pltpu.VMEM((n_sub, tq, LANES), F32)],
        compiler_params=_params("parallel", "parallel", "arbitrary"),
        name="diff_attention",
    )(*args)


def _top2_sum(a, b, c, d):
    hi1, lo1 = jnp.maximum(a, b), jnp.minimum(a, b)
    hi2, lo2 = jnp.maximum(c, d), jnp.minimum(c, d)
    return jnp.maximum(hi1, hi2) + jnp.maximum(jnp.minimum(hi1, hi2), jnp.maximum(lo1, lo2))


def _router_gates(h, rwt_ref, rbias_ref):
    tm = h.shape[0]
    h_hi, h_lo = _split_bf16(h)
    w_hi, w_lo = _split_bf16(rwt_ref[...])
    nt = (((1,), (1,)), ((), ()))
    logits = (lax.dot_general(w_hi, h_hi, nt, preferred_element_type=F32)
              + lax.dot_general(w_hi, h_lo, nt, preferred_element_type=F32)
              + lax.dot_general(w_lo, h_hi, nt, preferred_element_type=F32))
    scores = _sigmoid(logits)
    sel = scores + rbias_ref[...]
    srow = [sel[e:e + 1, :] for e in range(N_EXPERTS)]
    crow = [scores[e:e + 1, :] for e in range(N_EXPERTS)]
    gscore = [_top2_sum(*srow[EXPERTS_PER_GROUP * g:EXPERTS_PER_GROUP * (g + 1)]) for g in range(N_GROUPS)]
    best = jnp.zeros((1, tm), jnp.int32)
    best_v = gscore[0]
    for g in range(1, N_GROUPS):
        upd = gscore[g] > best_v
        best = jnp.where(upd, g, best)
        best_v = jnp.where(upd, gscore[g], best_v)
    cand, cand_score = [], []
    for j in range(EXPERTS_PER_GROUP):
        cs, cc = srow[j], crow[j]
        for g in range(1, N_GROUPS):
            cs = jnp.where(best == g, srow[EXPERTS_PER_GROUP * g + j], cs)
            cc = jnp.where(best == g, crow[EXPERTS_PER_GROUP * g + j], cc)
        cand.append(cs)
        cand_score.append(cc)
    i0 = jnp.zeros((1, tm), jnp.int32)
    v0, w0 = cand[0], cand_score[0]
    for j in range(1, EXPERTS_PER_GROUP):
        upd = cand[j] > v0
        i0 = jnp.where(upd, j, i0)
        v0 = jnp.where(upd, cand[j], v0)
        w0 = jnp.where(upd, cand_score[j], w0)
    i1 = jnp.zeros((1, tm), jnp.int32)
    v1 = jnp.full((1, tm), -jnp.inf, F32)
    w1 = jnp.zeros((1, tm), F32)
    for j in range(EXPERTS_PER_GROUP):
        upd = jnp.logical_and(i0 != j, cand[j] > v1)
        i1 = jnp.where(upd, j, i1)
        v1 = jnp.where(upd, cand[j], v1)
        w1 = jnp.where(upd, cand_score[j], w1)
    wsum = w0 + w1
    e0 = best * EXPERTS_PER_GROUP + i0
    e1 = best * EXPERTS_PER_GROUP + i1
    eid = lax.broadcasted_iota(jnp.int32, (N_EXPERTS, tm), 0)
    gates_t = jnp.where(eid == e0, w0 / wsum, 0.0) + jnp.where(eid == e1, w1 / wsum, 0.0)
    gates_t = jnp.concatenate([gates_t, jnp.zeros((LANES - N_EXPERTS, tm), F32)], axis=0)
    return gates_t.T


def _outproj_kernel(att_ref, g_ref, gprev_ref, gnext_ref, gb_ref, x_ref, m_ref, wo_ref, cw_ref, cb_ref,
                    lng_ref, lnb_ref, rwt_ref, rbias_ref, x1_ref, h_ref, gates_ref, *, n_tiles):
    i = pl.program_id(1)
    g = g_ref[...]
    tm = g.shape[0]
    rows = lax.broadcasted_iota(jnp.int32, g.shape, 0)
    prev_row = jnp.where(i > 0, gprev_ref[SUBLANES - 1:SUBLANES, :], 0.0)
    next_row = jnp.where(i < n_tiles - 1, gnext_ref[0:1, :], 0.0)
    g_m1 = jnp.where(rows == 0, prev_row, pltpu.roll(g, 1, 0))
    g_p1 = jnp.where(rows == tm - 1, next_row, pltpu.roll(g, tm - 1, 0))
    conv = cw_ref[0:1, :] * g_m1 + cw_ref[1:2, :] * g + cw_ref[2:3, :] * g_p1 + cb_ref[...]
    yc = (gb_ref[...] * conv).astype(BF16)
    y = (jnp.dot(att_ref[...], wo_ref[:D_ATT, :], preferred_element_type=F32)
         + jnp.dot(yc, wo_ref[D_ATT:, :], preferred_element_type=F32))
    x1 = _ln(ALPHA * x_ref[...] + m_ref[2:3, :] * y) * lng_ref[...] + lnb_ref[...]
    x1_ref[...] = x1
    h = _ln(x1) * (1.0 + m_ref[4:5, :]) + m_ref[3:4, :]
    h_ref[...] = h.astype(BF16)
    gates_ref[...] = _router_gates(h, rwt_ref, rbias_ref)


def _outproj(att, g, gb, x, mod_l, wo, cw, cb, lng, lnb, rwt, rbias, *, tm, cond_row):
    b, n, d = x.shape
    n_tiles = n // tm
    g8 = g.reshape(b, n // SUBLANES, SUBLANES, D_CONV)
    tpb = tm // SUBLANES
    row = lambda w: pl.BlockSpec((None, tm, w), lambda bi, i: (bi, i, 0))
    full = lambda s: pl.BlockSpec(s, lambda bi, i: (0,) * len(s))
    return pl.pallas_call(
        functools.partial(_outproj_kernel, n_tiles=n_tiles),
        grid=(b, n_tiles),
        in_specs=[
            row(D_ATT), row(D_CONV),
            pl.BlockSpec((None, None, SUBLANES, D_CONV), lambda bi, i: (bi, jnp.maximum(i * tpb - 1, 0), 0, 0)),
            pl.BlockSpec((None, None, SUBLANES, D_CONV),
                         lambda bi, i: (bi, jnp.minimum((i + 1) * tpb, n // SUBLANES - 1), 0, 0)),
            row(D_CONV), row(d),
            pl.BlockSpec((None, N_MOD_PAD, d), lambda bi, i: (cond_row(bi), 0, 0)),
            full((d, d)), full((3, D_CONV)), full((1, D_CONV)), full((1, d)), full((1, d)),
            full((N_EXPERTS, d)), full((N_EXPERTS, 1)),
        ],
        out_specs=[row(d), row(d), row(LANES)],
        out_shape=[jax.ShapeDtypeStruct((b, n, d), F32), jax.ShapeDtypeStruct((b, n, d), BF16),
                   jax.ShapeDtypeStruct((b, n, LANES), F32)],
        compiler_params=_params("parallel", "parallel"),
        name="outproj_ln1_router",
    )(att, g, g8, g8, gb, x, mod_l, wo, cw, cb, lng, lnb, rwt, rbias)


def _moe_dense_kernel(h_ref, gates_ref, wg_ref, wu_ref, wd_ref, x1_ref, m_ref, lng_ref, lnb_ref, o_ref, acc_ref):
    e = pl.program_id(1)

    @pl.when(e == 0)
    def _():
        acc_ref[...] = jnp.zeros_like(acc_ref)

    h = h_ref[...]
    gate = jnp.dot(h, wg_ref[...], preferred_element_type=F32)
    up = jnp.dot(h, wu_ref[...], preferred_element_type=F32)
    he = (gate * _sigmoid(gate) * up).astype(BF16)
    y = jnp.dot(he, wd_ref[...], preferred_element_type=F32)
    gts = gates_ref[...]
    lane = lax.broadcasted_iota(jnp.int32, gts.shape, 1)
    gcol = jnp.sum(jnp.where(lane == e, gts, 0.0), axis=1, keepdims=True)
    acc_ref[...] += gcol * y

    @pl.when(e == N_EXPERTS - 1)
    def _():
        o_ref[...] = _ln(ALPHA * x1_ref[...] + m_ref[5:6, :] * acc_ref[...]) * lng_ref[...] + lnb_ref[...]


def _moe_dense(h, gates, wg, wu, wd, x1, mod_l, lng, lnb, *, cond_row):
    n, d = h.shape
    tm = TM_MOE
    tok = lambda w: pl.BlockSpec((tm, w), lambda i, e: (i, 0))
    vec = pl.BlockSpec((1, d), lambda i, e: (0, 0))
    return pl.pallas_call(
        _moe_dense_kernel,
        grid=(n // tm, N_EXPERTS),
        in_specs=[
            tok(d), tok(LANES),
            pl.BlockSpec((None, d, D_EXPERT), lambda i, e: (e, 0, 0)),
            pl.BlockSpec((None, d, D_EXPERT), lambda i, e: (e, 0, 0)),
            pl.BlockSpec((None, D_EXPERT, d), lambda i, e: (e, 0, 0)),
            tok(d),
            pl.BlockSpec((None, N_MOD_PAD, d), lambda i, e: (cond_row(i), 0, 0)),
            vec, vec,
        ],
        out_specs=tok(d),
        out_shape=jax.ShapeDtypeStruct((n, d), F32),
        scratch_shapes=[pltpu.VMEM((tm, d), F32)],
        compiler_params=_params("parallel", "arbitrary"),
        name="moe_dense_ln2",
    )(h, gates, wg, wu, wd, x1, mod_l, lng, lnb)


def _rope_tables():
    t = jnp.arange(SEQ, dtype=jnp.int32)
    pos = jnp.stack([(t // GRID_W).astype(F32), (t % GRID_W).astype(F32)], axis=1)
    inv = 1.0 / (ROPE_THETA ** (jnp.arange(N_FREQ, dtype=F32) / N_FREQ))
    ang = pos[:, :, None] * inv
    cos, sin = jnp.cos(ang), jnp.sin(ang)
    zero = jnp.zeros_like(sin)
    expand = lambda first, second: jnp.tile(
        jnp.stack([first, second], axis=2).reshape(SEQ, DQK), (1, LANES // DQK))
    return expand(cos, cos), expand(-sin, zero), expand(zero, sin)


def kernel(x, c, ctx, c_ctx, w_mod, b_mod, w_in, diff_lambda, attn_norm_g, conv_w, conv_b, w_out,
           ln1_g, ln1_b, ln2_g, ln2_b, router_w, router_bias, w_gate, w_up, w_down):
    assert x.shape == (BATCH, SEQ, D_MODEL) and ctx.shape == (BATCH, CTX_LEN, D_MODEL)
    d = D_MODEL
    cc = jnp.concatenate([c, c_ctx[None, :], jnp.zeros((N_COND_PAD - N_COND, d), F32)], axis=0)
    mod = _modulation(cc, w_mod, b_mod)
    rope_tabs = _rope_tables()
    w_in_b, w_out_b = w_in.astype(BF16), w_out.astype(BF16)
    wg_b, wu_b, wd_b = w_gate.astype(BF16), w_up.astype(BF16), w_down.astype(BF16)
    rwt = router_w.T
    rbias = router_bias.reshape(N_EXPERTS, 1)

    lat_row = lambda bi: bi
    ctx_row = lambda bi: CTX_ROW
    lat_tile_row = lambda i: i // (SEQ // TM_MOE)

    for l in range(DEPTH):
        last = l == DEPTH - 1
        lam_init = 0.8 - 0.6 * math.exp(-0.3 * l)
        vec = lambda p, w: p[l].reshape(1, w)
        post = (w_out_b[l], conv_w[l], vec(conv_b, D_CONV), vec(ln1_g, d), vec(ln1_b, d), rwt, rbias)

        q, kt, v, g, gb = _inproj(x, mod[l], w_in_b[l], rope_tabs, tm=TM_LAT, cond_row=lat_row)
        if last:
            ktc, vc = _inproj(ctx, mod[l], w_in_b[l][:, D_ATT:3 * D_ATT], None, tm=TM_CTX, cond_row=ctx_row,
                              kv_only=True)
        else:
            qc, ktc, vc, gc, gbc = _inproj(ctx, mod[l], w_in_b[l], None, tm=TM_CTX, cond_row=ctx_row)

        att = _attention(diff_lambda[l], q, [(kt, v), (ktc, vc)], attn_norm_g[l], lam_init=lam_init)
        x1, h, gates = _outproj(att, g, gb, x, mod[l], *post, tm=TM_LAT, cond_row=lat_row)
        moe_w = (wg_b[l], wu_b[l], wd_b[l])
        x = _moe_dense(h.reshape(-1, d), gates.reshape(-1, LANES), *moe_w, x1.reshape(-1, d), mod[l],
                       vec(ln2_g, d), vec(ln2_b, d), cond_row=lat_tile_row).reshape(BATCH, SEQ, d)
        if not last:
            att_c = _attention(diff_lambda[l], qc, [(ktc, vc)], attn_norm_g[l], lam_init=lam_init)
            c1, hc, gates_c = _outproj(att_c, gc, gbc, ctx, mod[l], *post, tm=TM_CTX, cond_row=ctx_row)
            ctx = _moe_dense(hc.reshape(-1, d), gates_c.reshape(-1, LANES), *moe_w, c1.reshape(-1, d), mod[l],
                             vec(ln2_g, d), vec(ln2_b, d), cond_row=lambda i: CTX_ROW
                             ).reshape(BATCH, CTX_LEN, d)
    return x
```

```python
import functools
import math

import jax
import jax.numpy as jnp
from jax import lax
from jax.experimental import pallas as pl
from jax.experimental.pallas import tpu as pltpu

D_MODEL = 1024
BATCH = 16
SEQ = 4096
DEPTH = 2
GRID_W = 64
CTX_LEN = 256
D_ATT = D_MODEL // 2
D_CONV = D_MODEL - D_ATT
N_HEADS = 4
DV = D_ATT // N_HEADS
DQK = DV // 2
N_FREQ = DQK // 4
ROPE_THETA = 10000.0
N_EXPERTS = 16
N_GROUPS = 4
EXPERTS_PER_GROUP = N_EXPERTS // N_GROUPS
D_EXPERT = 512
LN_EPS = 1e-6
HEAD_NORM_EPS = 1e-5
ALPHA = (2 * DEPTH) ** 0.25

F32 = jnp.float32
BF16 = jnp.bfloat16

LANES = 128
SUBLANES = 8
VMEM_LIMIT_BYTES = 48 * 1024 * 1024

N_COND = BATCH + 1
N_COND_PAD = 24
CTX_ROW = BATCH
N_MOD = 6
N_MOD_PAD = SUBLANES

TM_LAT = 512
TM_CTX = CTX_LEN
TQ = 256
N_SUB = 2
RB = 16
LOG2E = math.log2(math.e)
PAIRS_PER_GROUP = EXPERTS_PER_GROUP * (EXPERTS_PER_GROUP - 1) // 2
N_BUCKETS = N_GROUPS * PAIRS_PER_GROUP
N_BUCKET_PAD = 32
HX_W = D_MODEL + LANES
TMS_LAT = 512
TMS_CTX = 256
DISPATCH_CHUNK = 2048
TM_COMBINE = 256


def _params(*sem):
    return pltpu.CompilerParams(dimension_semantics=sem, vmem_limit_bytes=VMEM_LIMIT_BYTES)


def _ln(x):
    mu = jnp.mean(x, axis=-1, keepdims=True)
    xc = x - mu
    var = jnp.mean(xc * xc, axis=-1, keepdims=True)
    return xc * lax.rsqrt(var + LN_EPS)


def _sigmoid(x):
    return 1.0 / (1.0 + jnp.exp(-x))


def _split_bf16(x):
    hi = x.astype(BF16)
    lo = (x - hi.astype(F32)).astype(BF16)
    return hi, lo


def _mod_kernel(cc_ref, w_ref, b_ref, o_ref):
    a = cc_ref[...]
    a = a * _sigmoid(a)
    a_hi, a_lo = _split_bf16(a)
    w_hi, w_lo = _split_bf16(w_ref[...])
    m = (jnp.dot(a_hi, w_hi, preferred_element_type=F32)
         + jnp.dot(a_lo, w_hi, preferred_element_type=F32)
         + jnp.dot(a_hi, w_lo, preferred_element_type=F32))
    o_ref[...] = m + b_ref[...]


def _modulation(cc, w_mod, b_mod):
    d = D_MODEL
    out = pl.pallas_call(
        _mod_kernel,
        grid=(DEPTH, N_MOD),
        in_specs=[
            pl.BlockSpec((N_COND_PAD, d), lambda l, j: (0, 0)),
            pl.BlockSpec((None, d, d), lambda l, j: (l, 0, j)),
            pl.BlockSpec((None, 1, d), lambda l, j: (l, 0, j)),
        ],
        out_specs=pl.BlockSpec((None, N_COND_PAD, d), lambda l, j: (l, 0, j)),
        out_shape=jax.ShapeDtypeStruct((DEPTH, N_COND_PAD, N_MOD * d), F32),
        compiler_params=_params("parallel", "parallel"),
        name="modulation",
    )(cc, w_mod, b_mod.reshape(DEPTH, 1, N_MOD * d))
    m = out.reshape(DEPTH, N_COND_PAD, N_MOD, d)
    return jnp.pad(m, ((0, 0), (0, 0), (0, N_MOD_PAD - N_MOD), (0, 0)))


def _inproj_kernel(*refs, use_rope, kv_only):
    x_ref, m_ref, w_ref = refs[:3]
    rest = refs[3:]
    if use_rope:
        ra_ref, rb_ref, rc_ref = rest[:3]
        rest = rest[3:]
    u = _ln(x_ref[...]) * (1.0 + m_ref[1:2, :]) + m_ref[0:1, :]
    y = jnp.dot(u.astype(BF16), w_ref[...], preferred_element_type=F32)
    if kv_only:
        kt_ref, v_ref = rest
        kt_ref[...] = y[:, :D_ATT].T.astype(BF16)
        v_ref[...] = y[:, D_ATT:2 * D_ATT].astype(BF16)
        return
    q_ref, kt_ref, v_ref, g_ref, gb_ref = rest
    q = y[:, :D_ATT] * (DQK ** -0.5 * LOG2E)
    k = y[:, D_ATT:2 * D_ATT]
    if use_rope:
        ra, rb, rc = ra_ref[...], rb_ref[...], rc_ref[...]

        def rope(t):
            cols = []
            for j in range(D_ATT // LANES):
                tj = t[:, j * LANES:(j + 1) * LANES]
                cols.append(tj * ra
                            + pltpu.roll(tj, LANES - N_FREQ, 1) * rb
                            + pltpu.roll(tj, N_FREQ, 1) * rc)
            return jnp.concatenate(cols, axis=1)

        q = rope(q)
        k = rope(k)
    q_ref[...] = q.astype(BF16)
    kt_ref[...] = k.T.astype(BF16)
    v_ref[...] = y[:, 2 * D_ATT:3 * D_ATT].astype(BF16)
    gb_ref[...] = y[:, 3 * D_ATT:3 * D_ATT + D_CONV]
    g_ref[...] = y[:, 3 * D_ATT + D_CONV:3 * D_ATT + 2 * D_CONV] * y[:, 3 * D_ATT + 2 * D_CONV:]


def _inproj(x, mod_l, w, rope_tabs, *, tm, cond_row, kv_only=False):
    b, n, d = x.shape
    use_rope = rope_tabs is not None
    wn = w.shape[1]
    in_specs = [
        pl.BlockSpec((None, tm, d), lambda bi, i: (bi, i, 0)),
        pl.BlockSpec((None, N_MOD_PAD, d), lambda bi, i: (cond_row(bi), 0, 0)),
        pl.BlockSpec((d, wn), lambda bi, i: (0, 0)),
    ]
    args = [x, mod_l, w]
    if use_rope:
        in_specs += [pl.BlockSpec((tm, LANES), lambda bi, i: (i, 0))] * 3
        args += list(rope_tabs)
    kt_spec = pl.BlockSpec((None, D_ATT, tm), lambda bi, i: (bi, 0, i))
    row_spec = pl.BlockSpec((None, tm, D_ATT), lambda bi, i: (bi, i, 0))
    kt_shape = jax.ShapeDtypeStruct((b, D_ATT, n), BF16)
    if kv_only:
        out_specs = [kt_spec, row_spec]
        out_shape = [kt_shape, jax.ShapeDtypeStruct((b, n, D_ATT), BF16)]
    else:
        out_specs = [row_spec, kt_spec, row_spec, row_spec, row_spec]
        out_shape = [jax.ShapeDtypeStruct((b, n, D_ATT), BF16), kt_shape,
                     jax.ShapeDtypeStruct((b, n, D_ATT), BF16),
                     jax.ShapeDtypeStruct((b, n, D_CONV), F32),
                     jax.ShapeDtypeStruct((b, n, D_CONV), F32)]
    return pl.pallas_call(
        functools.partial(_inproj_kernel, use_rope=use_rope, kv_only=kv_only),
        grid=(b, n // tm),
        in_specs=in_specs,
        out_specs=out_specs,
        out_shape=out_shape,
        compiler_params=_params("parallel", "parallel"),
        name="inproj",
    )(*args)


def _attn_kernel(*refs, n_seg, lam_init):
    lp_ref, q_ref = refs[:2]
    seg_refs = refs[2:2 + 2 * n_seg]
    gn_ref, o_ref, s_ref, a_ref, ml_ref, r_ref = refs[2 + 2 * n_seg:]
    kt_refs = seg_refs[0::2]
    v_refs = seg_refs[1::2]
    n_sub, two_tq, _ = s_ref.shape
    tq = two_tq // 2

    lp = lp_ref[...]
    lam = (jnp.exp(jnp.sum(lp[0:1, :] * lp[1:2, :], axis=-1, keepdims=True))
           - jnp.exp(jnp.sum(lp[2:3, :] * lp[3:4, :], axis=-1, keepdims=True)) + lam_init)

    def scores(j):
        q = q_ref[j * tq:(j + 1) * tq, :]
        lane = lax.broadcasted_iota(jnp.int32, q.shape, 1)
        zero = jnp.zeros_like(q)
        qz = jnp.concatenate([jnp.where(lane < DQK, q, zero), jnp.where(lane >= DQK, q, zero)], axis=0)
        off = 0
        for kt_ref in kt_refs:
            nk = kt_ref.shape[1]
            s_ref[j, :, off:off + nk] = jnp.dot(qz, kt_ref[...], preferred_element_type=F32)
            off += nk

    nk_total = s_ref.shape[2]
    n_rep = nk_total // LANES

    def softmax(j):
        for g in range(two_tq // SUBLANES):
            rows = slice(g * SUBLANES, (g + 1) * SUBLANES)
            m = jnp.max(s_ref[j, rows, :], axis=-1, keepdims=True)
            ml_ref[j, rows, :] = jnp.broadcast_to(m, (SUBLANES, LANES))
        for g in range(two_tq // SUBLANES):
            rows = slice(g * SUBLANES, (g + 1) * SUBLANES)
            e = jnp.exp2(s_ref[j, rows, :] - jnp.concatenate([ml_ref[j, rows, :]] * n_rep, axis=1))
            s_ref[j, rows, :] = e
            ml_ref[j, rows, :] = jnp.broadcast_to(jnp.sum(e, axis=-1, keepdims=True), (SUBLANES, LANES))
        for r in range(tq // RB):
            rows1 = slice(r * RB, (r + 1) * RB)
            rows2 = slice(tq + r * RB, tq + (r + 1) * RB)
            l1 = ml_ref[j, rows1, :]
            c = jnp.concatenate([lam * l1 / ml_ref[j, rows2, :]] * n_rep, axis=1)
            a_ref[j, rows1, :] = (s_ref[j, rows1, :] - c * s_ref[j, rows2, :]).astype(BF16)
            r_ref[j, rows1, :] = 1.0 / l1

    def values(j):
        o = None
        off = 0
        for v_ref in v_refs:
            nk = v_ref.shape[0]
            part = jnp.dot(a_ref[j, :, off:off + nk], v_ref[...], preferred_element_type=F32)
            o = part if o is None else o + part
            off += nk
        o = o * r_ref[j]
        y = o * lax.rsqrt(jnp.mean(o * o, axis=-1, keepdims=True) + HEAD_NORM_EPS)
        o_ref[j * tq:(j + 1) * tq, :] = (y * gn_ref[...] * (1.0 - lam_init)).astype(BF16)

    for j in range(n_sub):
        scores(j)
    for j in range(n_sub):
        softmax(j)
        values(j)


def _attention(lam_params, q, segs, attn_g, *, lam_init):
    b, n, _ = q.shape
    tq = min(TQ, n)
    n_sub = min(N_SUB, n // tq)
    tstep = n_sub * tq
    nk_total = sum(kt.shape[2] for kt, _ in segs)
    in_specs = [
        pl.BlockSpec((4, DQK), lambda bi, h, i: (0, 0)),
        pl.BlockSpec((None, tstep, DV), lambda bi, h, i: (bi, i, h)),
    ]
    args = [lam_params, q]
    for kt, v in segs:
        nk = kt.shape[2]
        in_specs += [pl.BlockSpec((None, DV, nk), lambda bi, h, i: (bi, h, 0)),
                     pl.BlockSpec((None, nk, DV), lambda bi, h, i: (bi, 0, h))]
        args += [kt, v]
    in_specs.append(pl.BlockSpec((1, DV), lambda bi, h, i: (0, h)))
    args.append(attn_g.reshape(1, D_ATT))
    return pl.pallas_call(
        functools.partial(_attn_kernel, n_seg=len(segs), lam_init=lam_init),
        grid=(b, N_HEADS, n // tstep),
        in_specs=in_specs,
        out_specs=pl.BlockSpec((None, tstep, DV), lambda bi, h, i: (bi, i, h)),
        out_shape=jax.ShapeDtypeStruct((b, n, D_ATT), BF16),
        scratch_shapes=[pltpu.VMEM((n_sub, 2 * tq, nk_total), F32),
                        pltpu.VMEM((n_sub, tq, nk_total), BF16),
                        pltpu.VMEM((n_sub, 2 * tq, LANES), F32),
                        pltpu.VMEM((n_sub, tq, LANES), F32)],
        compiler_params=_params("parallel", "parallel", "arbitrary"),
        name="diff_attention",
    )(*args)


def _top2_sum(a, b, c, d):
    hi1, lo1 = jnp.maximum(a, b), jnp.minimum(a, b)
    hi2, lo2 = jnp.maximum(c, d), jnp.minimum(c, d)
    return jnp.maximum(hi1, hi2) + jnp.maximum(jnp.minimum(hi1, hi2), jnp.maximum(lo1, lo2))


def _router(h, rwt_ref, rbias_ref):
    tm = h.shape[0]
    h_hi, h_lo = _split_bf16(h)
    w_hi, w_lo = _split_bf16(rwt_ref[...])
    nt = (((1,), (1,)), ((), ()))
    logits = (lax.dot_general(w_hi, h_hi, nt, preferred_element_type=F32)
              + lax.dot_general(w_hi, h_lo, nt, preferred_element_type=F32)
              + lax.dot_general(w_lo, h_hi, nt, preferred_element_type=F32))
    scores = _sigmoid(logits)
    sel = scores + rbias_ref[...]
    srow = [sel[e:e + 1, :] for e in range(N_EXPERTS)]
    crow = [scores[e:e + 1, :] for e in range(N_EXPERTS)]
    gscore = [_top2_sum(*srow[EXPERTS_PER_GROUP * g:EXPERTS_PER_GROUP * (g + 1)]) for g in range(N_GROUPS)]
    best = jnp.zeros((1, tm), jnp.int32)
    best_v = gscore[0]
    for g in range(1, N_GROUPS):
        upd = gscore[g] > best_v
        best = jnp.where(upd, g, best)
        best_v = jnp.where(upd, gscore[g], best_v)
    cand, cand_score = [], []
    for j in range(EXPERTS_PER_GROUP):
        cs, cc = srow[j], crow[j]
        for g in range(1, N_GROUPS):
            cs = jnp.where(best == g, srow[EXPERTS_PER_GROUP * g + j], cs)
            cc = jnp.where(best == g, crow[EXPERTS_PER_GROUP * g + j], cc)
        cand.append(cs)
        cand_score.append(cc)
    i0 = jnp.zeros((1, tm), jnp.int32)
    v0, w0 = cand[0], cand_score[0]
    for j in range(1, EXPERTS_PER_GROUP):
        upd = cand[j] > v0
        i0 = jnp.where(upd, j, i0)
        v0 = jnp.where(upd, cand[j], v0)
        w0 = jnp.where(upd, cand_score[j], w0)
    i1 = jnp.zeros((1, tm), jnp.int32)
    v1 = jnp.full((1, tm), -jnp.inf, F32)
    w1 = jnp.zeros((1, tm), F32)
    for j in range(EXPERTS_PER_GROUP):
        upd = jnp.logical_and(i0 != j, cand[j] > v1)
        i1 = jnp.where(upd, j, i1)
        v1 = jnp.where(upd, cand[j], v1)
        w1 = jnp.where(upd, cand_score[j], w1)
    wsum = w0 + w1
    first_lo = i0 < i1
    lo = jnp.where(first_lo, i0, i1)
    hi = jnp.where(first_lo, i1, i0)
    g_lo = jnp.where(first_lo, w0, w1) / wsum
    g_hi = jnp.where(first_lo, w1, w0) / wsum
    pair = jnp.where(lo == 0, hi - 1, jnp.where(lo == 1, hi + 1, PAIRS_PER_GROUP - 1))
    return best * PAIRS_PER_GROUP + pair, g_lo, g_hi


def _outproj_kernel(att_ref, g_ref, gprev_ref, gnext_ref, gb_ref, x_ref, m_ref, wo_ref, cw_ref, cb_ref,
                    lng_ref, lnb_ref, rwt_ref, rbias_ref, tri_ref, x1_ref, hx_ref, meta_ref, cnt_ref, run_ref,
                    *, n_tiles):
    i = pl.program_id(1)

    @pl.when(jnp.logical_and(pl.program_id(0) == 0, i == 0))
    def _():
        run_ref[...] = jnp.zeros_like(run_ref)

    g = g_ref[...]
    tm = g.shape[0]
    rows = lax.broadcasted_iota(jnp.int32, g.shape, 0)
    prev_row = jnp.where(i > 0, gprev_ref[SUBLANES - 1:SUBLANES, :], 0.0)
    next_row = jnp.where(i < n_tiles - 1, gnext_ref[0:1, :], 0.0)
    g_m1 = jnp.where(rows == 0, prev_row, pltpu.roll(g, 1, 0))
    g_p1 = jnp.where(rows == tm - 1, next_row, pltpu.roll(g, tm - 1, 0))
    conv = cw_ref[0:1, :] * g_m1 + cw_ref[1:2, :] * g + cw_ref[2:3, :] * g_p1 + cb_ref[...]
    yc = (gb_ref[...] * conv).astype(BF16)
    y = (jnp.dot(att_ref[...], wo_ref[:D_ATT, :], preferred_element_type=F32)
         + jnp.dot(yc, wo_ref[D_ATT:, :], preferred_element_type=F32))
    x1 = _ln(ALPHA * x_ref[...] + m_ref[2:3, :] * y) * lng_ref[...] + lnb_ref[...]
    x1_ref[...] = x1
    h = _ln(x1) * (1.0 + m_ref[4:5, :]) + m_ref[3:4, :]
    bucket, g_lo, g_hi = _router(h, rwt_ref, rbias_ref)
    onehot = lax.broadcasted_iota(jnp.int32, (N_BUCKET_PAD, tm), 0) == bucket
    onehot_f = jnp.where(onehot, 1.0, 0.0)
    csum = jnp.dot(onehot_f.astype(BF16), tri_ref[...], preferred_element_type=F32)
    run = run_ref[...]
    rank = jnp.sum(jnp.where(onehot, csum - 1.0 + run[:, 0:1], 0.0), axis=0, keepdims=True)
    run = run + jnp.sum(onehot_f, axis=1, keepdims=True)
    run_ref[...] = run
    cnt_ref[...] = run
    rows8 = lax.broadcasted_iota(jnp.int32, (SUBLANES, tm), 0)
    meta = jnp.where(rows8 == 0, bucket.astype(F32), jnp.where(rows8 == 1, rank, 0.0))
    meta_ref[...] = meta
    gate_rows = jnp.where(rows8 == 0, g_lo, jnp.where(rows8 == 1, g_hi, 0.0))
    gate_cols = jnp.concatenate([gate_rows, jnp.zeros((LANES - SUBLANES, tm), F32)], axis=0).T
    hx_ref[:, :D_MODEL] = h
    hx_ref[:, D_MODEL:] = gate_cols


def _outproj(att, g, gb, x, mod_l, wo, cw, cb, lng, lnb, rwt, rbias, tri, *, tm, cond_row):
    b, n, d = x.shape
    n_tiles = n // tm
    g8 = g.reshape(b, n // SUBLANES, SUBLANES, D_CONV)
    tpb = tm // SUBLANES
    row = lambda w: pl.BlockSpec((None, tm, w), lambda bi, i: (bi, i, 0))
    full = lambda s: pl.BlockSpec(s, lambda bi, i: (0,) * len(s))
    return pl.pallas_call(
        functools.partial(_outproj_kernel, n_tiles=n_tiles),
        grid=(b, n_tiles),
        in_specs=[
            row(D_ATT), row(D_CONV),
            pl.BlockSpec((None, None, SUBLANES, D_CONV), lambda bi, i: (bi, jnp.maximum(i * tpb - 1, 0), 0, 0)),
            pl.BlockSpec((None, None, SUBLANES, D_CONV),
                         lambda bi, i: (bi, jnp.minimum((i + 1) * tpb, n // SUBLANES - 1), 0, 0)),
            row(D_CONV), row(d),
            pl.BlockSpec((None, N_MOD_PAD, d), lambda bi, i: (cond_row(bi), 0, 0)),
            full((d, d)), full((3, D_CONV)), full((1, D_CONV)), full((1, d)), full((1, d)),
            full((N_EXPERTS, d)), full((N_EXPERTS, 1)), full((tm, tm)),
        ],
        out_specs=[row(d), row(HX_W), pl.BlockSpec((None, SUBLANES, tm), lambda bi, i: (bi, 0, i)),
                   full((N_BUCKET_PAD, LANES))],
        out_shape=[jax.ShapeDtypeStruct((b, n, d), F32), jax.ShapeDtypeStruct((b, n, HX_W), F32),
                   jax.ShapeDtypeStruct((b, SUBLANES, n), F32),
                   jax.ShapeDtypeStruct((N_BUCKET_PAD, LANES), F32)],
        scratch_shapes=[pltpu.VMEM((N_BUCKET_PAD, LANES), F32)],
        compiler_params=_params("arbitrary", "arbitrary"),
        name="outproj_ln1_router",
    )(att, g, g8, g8, gb, x, mod_l, wo, cw, cb, lng, lnb, rwt, rbias, tri)


def _dispatch_kernel(pad_start_ref, pad_len_ref, pos_ref, hx_hbm, xs_hbm, zero_ref, sem, zero_sem, *, chunk):
    step = pl.program_id(0)
    base = step * chunk

    def issue(i, carry):
        pltpu.make_async_copy(hx_hbm.at[pl.ds(base + i, 1)], xs_hbm.at[pl.ds(pos_ref[0, i], 1)], sem).start()
        return carry

    lax.fori_loop(0, chunk, issue, 0, unroll=8)

    @pl.when(step == 0)
    def _():
        zero_ref[...] = jnp.zeros_like(zero_ref)

        half = zero_ref.shape[0]
        n_tiles = xs_hbm.shape[0] // (2 * half)
        n_live = pad_start_ref[N_BUCKETS]

        def pad_copies(act):
            zero_copy = lambda start, rows: act(pltpu.make_async_copy(
                zero_ref.at[pl.ds(0, rows)], xs_hbm.at[pl.ds(start, rows)], zero_sem))
            for b in range(N_BUCKETS):
                start, length = pad_start_ref[b], pad_len_ref[b]
                head = jnp.minimum((-start) & (SUBLANES - 1), length)
                for k in range(SUBLANES - 1):
                    @pl.when(k < head)
                    def _(start=start, k=k):
                        zero_copy(start + k, 1)
                start = start + head
                length = length - head
                piece = half
                while piece >= SUBLANES:
                    @pl.when((length & piece) != 0)
                    def _(start=start, piece=piece):
                        zero_copy(pl.multiple_of(start, SUBLANES), piece)
                    start = start + (length & piece)
                    piece //= 2
            for k in range(N_BUCKETS):
                @pl.when(n_tiles - 1 - k >= n_live)
                def _(k=k):
                    zero_copy((n_tiles - 1 - k) * 2 * half, half)
                    zero_copy((n_tiles - 1 - k) * 2 * half + half, half)

        pad_copies(lambda copy: copy.start())
        pad_copies(lambda copy: copy.wait())

    pltpu.make_async_copy(hx_hbm.at[pl.ds(0, chunk)], xs_hbm.at[pl.ds(0, chunk)], sem).wait()


def _dispatch(pos, pad_start, pad_len, hx, n_rows, *, tms):
    n, w = hx.shape
    chunk = DISPATCH_CHUNK
    return pl.pallas_call(
        functools.partial(_dispatch_kernel, chunk=chunk),
        grid_spec=pltpu.PrefetchScalarGridSpec(
            num_scalar_prefetch=2,
            grid=(n // chunk,),
            in_specs=[pl.BlockSpec((None, 1, chunk), lambda i, ps, pn: (i, 0, 0), memory_space=pltpu.SMEM),
                      pl.BlockSpec(memory_space=pl.ANY)],
            out_specs=pl.BlockSpec(memory_space=pl.ANY),
            scratch_shapes=[pltpu.VMEM((tms // 2, w), F32), pltpu.SemaphoreType.DMA(()),
                            pltpu.SemaphoreType.DMA(())],
        ),
        out_shape=jax.ShapeDtypeStruct((n_rows, w), F32),
        compiler_params=_params("arbitrary"),
        name="moe_dispatch",
    )(pad_start, pad_len, pos.reshape(n // chunk, 1, chunk), hx)


def _experts_kernel(blk_ref, ea_ref, eb_ref, nv_ref, xs_ref, wga_ref, wua_ref, wda_ref, wgb_ref, wub_ref, wdb_ref,
                    ys_ref):
    nv = nv_ref[pl.program_id(0)]

    @pl.when(nv == 0)
    def _():
        ys_ref[...] = jnp.zeros_like(ys_ref)

    @pl.when(nv > 0)
    def _():
        x = xs_ref[:, :D_MODEL].astype(BF16)
        g_lo = xs_ref[:, D_MODEL:D_MODEL + 1]
        g_hi = xs_ref[:, D_MODEL + 1:D_MODEL + 2]

        def ffn(wg_ref, wu_ref, wd_ref):
            gate = jnp.dot(x, wg_ref[...], preferred_element_type=F32)
            up = jnp.dot(x, wu_ref[...], preferred_element_type=F32)
            he = (gate * _sigmoid(gate) * up).astype(BF16)
            return jnp.dot(he, wd_ref[...], preferred_element_type=F32)

        ys_ref[...] = g_lo * ffn(wga_ref, wua_ref, wda_ref) + g_hi * ffn(wgb_ref, wub_ref, wdb_ref)


def _experts(tile_blk, tile_ea, tile_eb, tile_nv, xs, wg, wu, wd, *, tms):
    n_rows, w = xs.shape
    d = D_MODEL
    n_tiles = n_rows // tms
    wspec = lambda shape, which: pl.BlockSpec(
        (None,) + shape, lambda j, blk, ea, eb, nv: ((ea, eb)[which][j], 0, 0))
    return pl.pallas_call(
        _experts_kernel,
        grid_spec=pltpu.PrefetchScalarGridSpec(
            num_scalar_prefetch=4,
            grid=(n_tiles,),
            in_specs=[
                pl.BlockSpec((tms, w), lambda j, blk, ea, eb, nv: (blk[j], 0)),
                wspec((d, D_EXPERT), 0), wspec((d, D_EXPERT), 0), wspec((D_EXPERT, d), 0),
                wspec((d, D_EXPERT), 1), wspec((d, D_EXPERT), 1), wspec((D_EXPERT, d), 1),
            ],
            out_specs=pl.BlockSpec((tms, d), lambda j, blk, ea, eb, nv: (j, 0)),
        ),
        out_shape=jax.ShapeDtypeStruct((n_rows, d), F32),
        compiler_params=_params("arbitrary"),
        name="moe_experts",
    )(tile_blk, tile_ea, tile_eb, tile_nv, xs, wg, wu, wd, wg, wu, wd)


def _combine_kernel(pos_ref, ys_hbm, x1_ref, m_ref, lng_ref, lnb_ref, o_ref, buf_ref, sem):
    tm = buf_ref.shape[0]

    def issue(i, carry):
        pltpu.make_async_copy(ys_hbm.at[pl.ds(pos_ref[0, i], 1)], buf_ref.at[pl.ds(i, 1)], sem).start()
        return carry

    lax.fori_loop(0, tm, issue, 0, unroll=8)
    pltpu.make_async_copy(ys_hbm.at[pl.ds(0, tm)], buf_ref, sem).wait()
    o_ref[...] = _ln(ALPHA * x1_ref[...] + m_ref[5:6, :] * buf_ref[...]) * lng_ref[...] + lnb_ref[...]


def _combine(pos, ys, x1, mod_l, lng, lnb, *, cond_row):
    n, d = x1.shape
    tm = TM_COMBINE
    vec = pl.BlockSpec((1, d), lambda i: (0, 0))
    return pl.pallas_call(
        _combine_kernel,
        grid=(n // tm,),
        in_specs=[
            pl.BlockSpec((None, 1, tm), lambda i: (i, 0, 0), memory_space=pltpu.SMEM),
            pl.BlockSpec(memory_space=pl.ANY),
            pl.BlockSpec((tm, d), lambda i: (i, 0)),
            pl.BlockSpec((None, N_MOD_PAD, d), lambda i: (cond_row(i), 0, 0)),
            vec, vec,
        ],
        out_specs=pl.BlockSpec((tm, d), lambda i: (i, 0)),
        out_shape=jax.ShapeDtypeStruct((n, d), F32),
        scratch_shapes=[pltpu.VMEM((tm, d), F32), pltpu.SemaphoreType.DMA(())],
        compiler_params=_params("arbitrary"),
        name="moe_combine_ln2",
    )(pos.reshape(n // tm, 1, tm), ys, x1, mod_l, lng, lnb)


def _routing_plan(meta, counts, *, tms):
    b, _, n = meta.shape
    n_tok = b * n
    bucket = meta[:, 0, :].reshape(n_tok).astype(jnp.int32)
    rank = meta[:, 1, :].reshape(n_tok).astype(jnp.int32)
    cnt = counts[:N_BUCKETS, 0].astype(jnp.int32)
    tiles = (cnt + tms - 1) // tms
    tile_end = jnp.cumsum(tiles)
    tile_off = tile_end - tiles
    n_tiles = n_tok // tms + N_BUCKETS
    ids = jnp.arange(N_BUCKETS, dtype=jnp.int32)
    pos = rank + jnp.sum(jnp.where(bucket[:, None] == ids[None, :], (tile_off * tms)[None, :], 0), axis=1)
    j = jnp.arange(n_tiles, dtype=jnp.int32)
    live = j < tile_end[-1]
    jb = jnp.minimum(j, tile_end[-1] - 1)
    tb = jnp.sum((jb[:, None] >= tile_end[None, :]).astype(jnp.int32), axis=1)
    pick = lambda table: jnp.sum(jnp.where(tb[:, None] == ids[None, :], table[None, :], 0), axis=1)
    nv = jnp.where(live, jnp.clip(pick(cnt) - (jb - pick(tile_off)) * tms, 0, tms), 0)
    pair = ids % PAIRS_PER_GROUP
    lo = jnp.where(pair < 3, 0, jnp.where(pair < 5, 1, 2))
    hi = jnp.where(pair < 3, pair + 1, jnp.where(pair < 5, pair - 1, 3))
    group = ids // PAIRS_PER_GROUP
    ea = pick(group * EXPERTS_PER_GROUP + lo)
    eb = pick(group * EXPERTS_PER_GROUP + hi)
    pad_start = jnp.concatenate([tile_off * tms + cnt, tile_end[-1:]]).astype(jnp.int32)
    pad_len = jnp.concatenate([tiles * tms - cnt, jnp.zeros((1,), jnp.int32)]).astype(jnp.int32)
    return pos, (pad_start, pad_len), (jb, ea, eb, nv.astype(jnp.int32)), n_tiles * tms


def _moe(hx, meta, counts, x1, moe_w, mod_l, lng, lnb, *, tms, cond_row):
    d = D_MODEL
    pos, pads, tile_tables, n_rows = _routing_plan(meta, counts, tms=tms)
    xs = _dispatch(pos, *pads, hx.reshape(-1, HX_W), n_rows, tms=tms)
    ys = _experts(*tile_tables, xs, *moe_w, tms=tms)
    return _combine(pos, ys, x1.reshape(-1, d), mod_l, lng, lnb, cond_row=cond_row)


def _rope_tables():
    t = jnp.arange(SEQ, dtype=jnp.int32)
    pos = jnp.stack([(t // GRID_W).astype(F32), (t % GRID_W).astype(F32)], axis=1)
    inv = 1.0 / (ROPE_THETA ** (jnp.arange(N_FREQ, dtype=F32) / N_FREQ))
    ang = pos[:, :, None] * inv
    cos, sin = jnp.cos(ang), jnp.sin(ang)
    zero = jnp.zeros_like(sin)
    expand = lambda first, second: jnp.tile(
        jnp.stack([first, second], axis=2).reshape(SEQ, DQK), (1, LANES // DQK))
    return expand(cos, cos), expand(-sin, zero), expand(zero, sin)


def kernel(x, c, ctx, c_ctx, w_mod, b_mod, w_in, diff_lambda, attn_norm_g, conv_w, conv_b, w_out,
           ln1_g, ln1_b, ln2_g, ln2_b, router_w, router_bias, w_gate, w_up, w_down):
    assert x.shape == (BATCH, SEQ, D_MODEL) and ctx.shape == (BATCH, CTX_LEN, D_MODEL)
    d = D_MODEL
    cc = jnp.concatenate([c, c_ctx[None, :], jnp.zeros((N_COND_PAD - N_COND, d), F32)], axis=0)
    mod = _modulation(cc, w_mod, b_mod)
    rope_tabs = _rope_tables()
    w_in_b, w_out_b = w_in.astype(BF16), w_out.astype(BF16)
    wg_b, wu_b, wd_b = w_gate.astype(BF16), w_up.astype(BF16), w_down.astype(BF16)
    rwt = router_w.T
    rbias = router_bias.reshape(N_EXPERTS, 1)

    lat_row = lambda bi: bi
    ctx_row = lambda bi: CTX_ROW
    lat_tile_row = lambda i: i // (SEQ // TM_COMBINE)
    tri = lambda tm: (jnp.arange(tm)[:, None] <= jnp.arange(tm)[None, :]).astype(BF16)

    for l in range(DEPTH):
        last = l == DEPTH - 1
        lam_init = 0.8 - 0.6 * math.exp(-0.3 * l)
        vec = lambda p, w: p[l].reshape(1, w)
        post = (w_out_b[l], conv_w[l], vec(conv_b, D_CONV), vec(ln1_g, d), vec(ln1_b, d), rwt, rbias)
        ln2 = (vec(ln2_g, d), vec(ln2_b, d))

        q, kt, v, g, gb = _inproj(x, mod[l], w_in_b[l], rope_tabs, tm=TM_LAT, cond_row=lat_row)
        if last:
            ktc, vc = _inproj(ctx, mod[l], w_in_b[l][:, D_ATT:3 * D_ATT], None, tm=TM_CTX, cond_row=ctx_row,
                              kv_only=True)
        else:
            qc, ktc, vc, gc, gbc = _inproj(ctx, mod[l], w_in_b[l], None, tm=TM_CTX, cond_row=ctx_row)

        att = _attention(diff_lambda[l], q, [(kt, v), (ktc, vc)], attn_norm_g[l], lam_init=lam_init)
        x1, hx, meta, counts = _outproj(att, g, gb, x, mod[l], *post, tri(TM_LAT), tm=TM_LAT, cond_row=lat_row)
        moe_w = (wg_b[l], wu_b[l], wd_b[l])
        x = _moe(hx, meta, counts, x1, moe_w, mod[l], *ln2, tms=TMS_LAT, cond_row=lat_tile_row
                 ).reshape(BATCH, SEQ, d)
        if not last:
            att_c = _attention(diff_lambda[l], qc, [(ktc, vc)], attn_norm_g[l], lam_init=lam_init)
            c1, hxc, meta_c, counts_c = _outproj(att_c, gc, gbc, ctx, mod[l], *post, tri(TM_CTX), tm=TM_CTX,
                                                 cond_row=ctx_row)
            ctx = _moe(hxc, meta_c, counts_c, c1, moe_w, mod[l], *ln2, tms=TMS_CTX, cond_row=ctx_row
                       ).reshape(BATCH, CTX_LEN, d)
    return x
```

```python
import functools
import math

import jax
import jax.numpy as jnp
from jax import lax
from jax.experimental import pallas as pl
from jax.experimental.pallas import tpu as pltpu

D_MODEL = 1024
BATCH = 16
SEQ = 4096
DEPTH = 2
GRID_W = 64
CTX_LEN = 256
D_ATT = D_MODEL // 2
D_CONV = D_MODEL - D_ATT
N_HEADS = 4
DV = D_ATT // N_HEADS
DQK = DV // 2
N_FREQ = DQK // 4
ROPE_THETA = 10000.0
N_EXPERTS = 16
N_GROUPS = 4
EXPERTS_PER_GROUP = N_EXPERTS // N_GROUPS
D_EXPERT = 512
LN_EPS = 1e-6
HEAD_NORM_EPS = 1e-5
ALPHA = (2 * DEPTH) ** 0.25

F32 = jnp.float32
BF16 = jnp.bfloat16

LANES = 128
SUBLANES = 8
VMEM_LIMIT_BYTES = 48 * 1024 * 1024

N_COND = BATCH + 1
N_COND_PAD = 24
CTX_ROW = BATCH
N_MOD = 6
N_MOD_PAD = SUBLANES

TM_LAT = 512
TM_CTX = CTX_LEN
TQ = 256
N_SUB = 2
RB = 16
LOG2E = math.log2(math.e)
PAIRS_PER_GROUP = EXPERTS_PER_GROUP * (EXPERTS_PER_GROUP - 1) // 2
N_BUCKETS = N_GROUPS * PAIRS_PER_GROUP
N_BUCKET_PAD = 32
HX_W = D_MODEL + LANES
TMS_LAT = 512
TMS_CTX = 256
DISPATCH_CHUNK = 2048
TM_COMBINE = 256


def _params(*sem):
    return pltpu.CompilerParams(dimension_semantics=sem, vmem_limit_bytes=VMEM_LIMIT_BYTES)


def _ln(x):
    mu = jnp.mean(x, axis=-1, keepdims=True)
    xc = x - mu
    var = jnp.mean(xc * xc, axis=-1, keepdims=True)
    return xc * lax.rsqrt(var + LN_EPS)


def _sigmoid(x):
    return 1.0 / (1.0 + jnp.exp(-x))


def _split_bf16(x):
    hi = x.astype(BF16)
    lo = (x - hi.astype(F32)).astype(BF16)
    return hi, lo


def _mod_kernel(cc_ref, w_ref, b_ref, o_ref):
    a = cc_ref[...]
    a = a * _sigmoid(a)
    a_hi, a_lo = _split_bf16(a)
    w_hi, w_lo = _split_bf16(w_ref[...])
    m = (jnp.dot(a_hi, w_hi, preferred_element_type=F32)
         + jnp.dot(a_lo, w_hi, preferred_element_type=F32)
         + jnp.dot(a_hi, w_lo, preferred_element_type=F32))
    o_ref[...] = m + b_ref[...]


def _modulation(cc, w_mod, b_mod):
    d = D_MODEL
    out = pl.pallas_call(
        _mod_kernel,
        grid=(DEPTH, N_MOD),
        in_specs=[
            pl.BlockSpec((N_COND_PAD, d), lambda l, j: (0, 0)),
            pl.BlockSpec((None, d, d), lambda l, j: (l, 0, j)),
            pl.BlockSpec((None, 1, d), lambda l, j: (l, 0, j)),
        ],
        out_specs=pl.BlockSpec((None, N_COND_PAD, d), lambda l, j: (l, 0, j)),
        out_shape=jax.ShapeDtypeStruct((DEPTH, N_COND_PAD, N_MOD * d), F32),
        compiler_params=_params("parallel", "parallel"),
        name="modulation",
    )(cc, w_mod, b_mod.reshape(DEPTH, 1, N_MOD * d))
    m = out.reshape(DEPTH, N_COND_PAD, N_MOD, d)
    return jnp.pad(m, ((0, 0), (0, 0), (0, N_MOD_PAD - N_MOD), (0, 0)))


def _inproj_kernel(*refs, use_rope, kv_only):
    x_ref, m_ref, w_ref = refs[:3]
    rest = refs[3:]
    if use_rope:
        ra_ref, rb_ref, rc_ref = rest[:3]
        rest = rest[3:]
    u = _ln(x_ref[...]) * (1.0 + m_ref[1:2, :]) + m_ref[0:1, :]
    y = jnp.dot(u.astype(BF16), w_ref[...], preferred_element_type=F32)
    if kv_only:
        kt_ref, v_ref = rest
        kt_ref[...] = y[:, :D_ATT].T.astype(BF16)
        v_ref[...] = y[:, D_ATT:2 * D_ATT].astype(BF16)
        return
    q_ref, kt_ref, v_ref, g_ref, gb_ref = rest
    q = y[:, :D_ATT] * (DQK ** -0.5 * LOG2E)
    k = y[:, D_ATT:2 * D_ATT]
    if use_rope:
        ra, rb, rc = ra_ref[...], rb_ref[...], rc_ref[...]

        def rope(t):
            cols = []
            for j in range(D_ATT // LANES):
                tj = t[:, j * LANES:(j + 1) * LANES]
                cols.append(tj * ra
                            + pltpu.roll(tj, LANES - N_FREQ, 1) * rb
                            + pltpu.roll(tj, N_FREQ, 1) * rc)
            return jnp.concatenate(cols, axis=1)

        q = rope(q)
        k = rope(k)
    q_ref[...] = q.astype(BF16)
    kt_ref[...] = k.T.astype(BF16)
    v_ref[...] = y[:, 2 * D_ATT:3 * D_ATT].astype(BF16)
    gb_ref[...] = y[:, 3 * D_ATT:3 * D_ATT + D_CONV]
    g_ref[...] = y[:, 3 * D_ATT + D_CONV:3 * D_ATT + 2 * D_CONV] * y[:, 3 * D_ATT + 2 * D_CONV:]


def _inproj(x, mod_l, w, rope_tabs, *, tm, cond_row, kv_only=False):
    b, n, d = x.shape
    use_rope = rope_tabs is not None
    wn = w.shape[1]
    in_specs = [
        pl.BlockSpec((None, tm, d), lambda bi, i: (bi, i, 0)),
        pl.BlockSpec((None, N_MOD_PAD, d), lambda bi, i: (cond_row(bi), 0, 0)),
        pl.BlockSpec((d, wn), lambda bi, i: (0, 0)),
    ]
    args = [x, mod_l, w]
    if use_rope:
        in_specs += [pl.BlockSpec((tm, LANES), lambda bi, i: (i, 0))] * 3
        args += list(rope_tabs)
    kt_spec = pl.BlockSpec((None, D_ATT, tm), lambda bi, i: (bi, 0, i))
    row_spec = pl.BlockSpec((None, tm, D_ATT), lambda bi, i: (bi, i, 0))
    kt_shape = jax.ShapeDtypeStruct((b, D_ATT, n), BF16)
    if kv_only:
        out_specs = [kt_spec, row_spec]
        out_shape = [kt_shape, jax.ShapeDtypeStruct((b, n, D_ATT), BF16)]
    else:
        out_specs = [row_spec, kt_spec, row_spec, row_spec, row_spec]
        out_shape = [jax.ShapeDtypeStruct((b, n, D_ATT), BF16), kt_shape,
                     jax.ShapeDtypeStruct((b, n, D_ATT), BF16),
                     jax.ShapeDtypeStruct((b, n, D_CONV), F32),
                     jax.ShapeDtypeStruct((b, n, D_CONV), F32)]
    return pl.pallas_call(
        functools.partial(_inproj_kernel, use_rope=use_rope, kv_only=kv_only),
        grid=(b, n // tm),
        in_specs=in_specs,
        out_specs=out_specs,
        out_shape=out_shape,
        compiler_params=_params("parallel", "parallel"),
        name="inproj",
    )(*args)


def _attn_kernel(*refs, n_seg, lam_init):
    lp_ref, q_ref = refs[:2]
    seg_refs = refs[2:2 + 2 * n_seg]
    gn_ref, o_ref, s_ref, a_ref, ml_ref, r_ref = refs[2 + 2 * n_seg:]
    kt_refs = seg_refs[0::2]
    v_refs = seg_refs[1::2]
    n_sub, two_tq, _ = s_ref.shape
    tq = two_tq // 2

    lp = lp_ref[...]
    lam = (jnp.exp(jnp.sum(lp[0:1, :] * lp[1:2, :], axis=-1, keepdims=True))
           - jnp.exp(jnp.sum(lp[2:3, :] * lp[3:4, :], axis=-1, keepdims=True)) + lam_init)

    def scores(j):
        q = q_ref[j * tq:(j + 1) * tq, :]
        lane = lax.broadcasted_iota(jnp.int32, q.shape, 1)
        zero = jnp.zeros_like(q)
        qz = jnp.concatenate([jnp.where(lane < DQK, q, zero), jnp.where(lane >= DQK, q, zero)], axis=0)
        off = 0
        for kt_ref in kt_refs:
            nk = kt_ref.shape[1]
            s_ref[j, :, off:off + nk] = jnp.dot(qz, kt_ref[...], preferred_element_type=F32)
            off += nk

    nk_total = s_ref.shape[2]
    n_rep = nk_total // LANES

    def softmax(j):
        for g in range(two_tq // SUBLANES):
            rows = slice(g * SUBLANES, (g + 1) * SUBLANES)
            m = jnp.max(s_ref[j, rows, :], axis=-1, keepdims=True)
            ml_ref[j, rows, :] = jnp.broadcast_to(m, (SUBLANES, LANES))
        for g in range(two_tq // SUBLANES):
            rows = slice(g * SUBLANES, (g + 1) * SUBLANES)
            e = jnp.exp2(s_ref[j, rows, :] - jnp.concatenate([ml_ref[j, rows, :]] * n_rep, axis=1))
            s_ref[j, rows, :] = e
            ml_ref[j, rows, :] = jnp.broadcast_to(jnp.sum(e, axis=-1, keepdims=True), (SUBLANES, LANES))
        for r in range(tq // RB):
            rows1 = slice(r * RB, (r + 1) * RB)
            rows2 = slice(tq + r * RB, tq + (r + 1) * RB)
            l1 = ml_ref[j, rows1, :]
            c = jnp.concatenate([lam * l1 / ml_ref[j, rows2, :]] * n_rep, axis=1)
            a_ref[j, rows1, :] = (s_ref[j, rows1, :] - c * s_ref[j, rows2, :]).astype(BF16)
            r_ref[j, rows1, :] = 1.0 / l1

    def values(j):
        o = None
        off = 0
        for v_ref in v_refs:
            nk = v_ref.shape[0]
            part = jnp.dot(a_ref[j, :, off:off + nk], v_ref[...], preferred_element_type=F32)
            o = part if o is None else o + part
            off += nk
        o = o * r_ref[j]
        y = o * lax.rsqrt(jnp.mean(o * o, axis=-1, keepdims=True) + HEAD_NORM_EPS)
        o_ref[j * tq:(j + 1) * tq, :] = (y * gn_ref[...] * (1.0 - lam_init)).astype(BF16)

    for j in range(n_sub):
        scores(j)
    for j in range(n_sub):
        softmax(j)
        values(j)


def _attention(lam_params, q, segs, attn_g, *, lam_init):
    b, n, _ = q.shape
    tq = min(TQ, n)
    n_sub = min(N_SUB, n // tq)
    tstep = n_sub * tq
    nk_total = sum(kt.shape[2] for kt, _ in segs)
    in_specs = [
        pl.BlockSpec((4, DQK), lambda bi, h, i: (0, 0)),
        pl.BlockSpec((None, tstep, DV), lambda bi, h, i: (bi, i, h)),
    ]
    args = [lam_params, q]
    for kt, v in segs:
        nk = kt.shape[2]
        in_specs += [pl.BlockSpec((None, DV, nk), lambda bi, h, i: (bi, h, 0)),
                     pl.BlockSpec((None, nk, DV), lambda bi, h, i: (bi, 0, h))]
        args += [kt, v]
    in_specs.append(pl.BlockSpec((1, DV), lambda bi, h, i: (0, h)))
    args.append(attn_g.reshape(1, D_ATT))
    return pl.pallas_call(
        functools.partial(_attn_kernel, n_seg=len(segs), lam_init=lam_init),
        grid=(b, N_HEADS, n // tstep),
        in_specs=in_specs,
        out_specs=pl.BlockSpec((None, tstep, DV), lambda bi, h, i: (bi, i, h)),
        out_shape=jax.ShapeDtypeStruct((b, n, D_ATT), BF16),
        scratch_shapes=[pltpu.VMEM((n_sub, 2 * tq, nk_total), F32),
                        pltpu.VMEM((n_sub, tq, nk_total), BF16),
                        pltpu.VMEM((n_sub, 2 * tq, LANES), F32),
                        pltpu.VMEM((n_sub, tq, LANES), F32)],
        compiler_params=_params("parallel", "parallel", "arbitrary"),
        name="diff_attention",
    )(*args)


def _top2_sum(a, b, c, d):
    hi1, lo1 = jnp.maximum(a, b), jnp.minimum(a, b)
    hi2, lo2 = jnp.maximum(c, d), jnp.minimum(c, d)
    return jnp.maximum(hi1, hi2) + jnp.maximum(jnp.minimum(hi1, hi2), jnp.maximum(lo1, lo2))


def _router(h, rwt_ref, rbias_ref):
    tm = h.shape[0]
    h_hi, h_lo = _split_bf16(h)
    w_hi, w_lo = _split_bf16(rwt_ref[...])
    nt = (((1,), (1,)), ((), ()))
    logits = (lax.dot_general(w_hi, h_hi, nt, preferred_element_type=F32)
              + lax.dot_general(w_hi, h_lo, nt, preferred_element_type=F32)
              + lax.dot_general(w_lo, h_hi, nt, preferred_element_type=F32))
    scores = _sigmoid(logits)
    sel = scores + rbias_ref[...]
    srow = [sel[e:e + 1, :] for e in range(N_EXPERTS)]
    crow = [scores[e:e + 1, :] for e in range(N_EXPERTS)]
    gscore = [_top2_sum(*srow[EXPERTS_PER_GROUP * g:EXPERTS_PER_GROUP * (g + 1)]) for g in range(N_GROUPS)]
    best = jnp.zeros((1, tm), jnp.int32)
    best_v = gscore[0]
    for g in range(1, N_GROUPS):
        upd = gscore[g] > best_v
        best = jnp.where(upd, g, best)
        best_v = jnp.where(upd, gscore[g], best_v)
    cand, cand_score = [], []
    for j in range(EXPERTS_PER_GROUP):
        cs, cc = srow[j], crow[j]
        for g in range(1, N_GROUPS):
            cs = jnp.where(best == g, srow[EXPERTS_PER_GROUP * g + j], cs)
            cc = jnp.where(best == g, crow[EXPERTS_PER_GROUP * g + j], cc)
        cand.append(cs)
        cand_score.append(cc)
    i0 = jnp.zeros((1, tm), jnp.int32)
    v0, w0 = cand[0], cand_score[0]
    for j in range(1, EXPERTS_PER_GROUP):
        upd = cand[j] > v0
        i0 = jnp.where(upd, j, i0)
        v0 = jnp.where(upd, cand[j], v0)
        w0 = jnp.where(upd, cand_score[j], w0)
    i1 = jnp.zeros((1, tm), jnp.int32)
    v1 = jnp.full((1, tm), -jnp.inf, F32)
    w1 = jnp.zeros((1, tm), F32)
    for j in range(EXPERTS_PER_GROUP):
        upd = jnp.logical_and(i0 != j, cand[j] > v1)
        i1 = jnp.where(upd, j, i1)
        v1 = jnp.where(upd, cand[j], v1)
        w1 = jnp.where(upd, cand_score[j], w1)
    wsum = w0 + w1
    first_lo = i0 < i1
    lo = jnp.where(first_lo, i0, i1)
    hi = jnp.where(first_lo, i1, i0)
    g_lo = jnp.where(first_lo, w0, w1) / wsum
    g_hi = jnp.where(first_lo, w1, w0) / wsum
    pair = jnp.where(lo == 0, hi - 1, jnp.where(lo == 1, hi + 1, PAIRS_PER_GROUP - 1))
    return best * PAIRS_PER_GROUP + pair, g_lo, g_hi


def _outproj_kernel(att_ref, g_ref, gprev_ref, gnext_ref, gb_ref, x_ref, m_ref, wo_ref, cw_ref, cb_ref,
                    lng_ref, lnb_ref, rwt_ref, rbias_ref, tri_ref, x1_ref, hx_ref, meta_ref, cnt_ref, run_ref,
                    *, n_tiles):
    i = pl.program_id(1)

    @pl.when(jnp.logical_and(pl.program_id(0) == 0, i == 0))
    def _():
        run_ref[...] = jnp.zeros_like(run_ref)

    g = g_ref[...]
    tm = g.shape[0]
    rows = lax.broadcasted_iota(jnp.int32, g.shape, 0)
    prev_row = jnp.where(i > 0, gprev_ref[SUBLANES - 1:SUBLANES, :], 0.0)
    next_row = jnp.where(i < n_tiles - 1, gnext_ref[0:1, :], 0.0)
    g_m1 = jnp.where(rows == 0, prev_row, pltpu.roll(g, 1, 0))
    g_p1 = jnp.where(rows == tm - 1, next_row, pltpu.roll(g, tm - 1, 0))
    conv = cw_ref[0:1, :] * g_m1 + cw_ref[1:2, :] * g + cw_ref[2:3, :] * g_p1 + cb_ref[...]
    yc = (gb_ref[...] * conv).astype(BF16)
    y = (jnp.dot(att_ref[...], wo_ref[:D_ATT, :], preferred_element_type=F32)
         + jnp.dot(yc, wo_ref[D_ATT:, :], preferred_element_type=F32))
    x1 = _ln(ALPHA * x_ref[...] + m_ref[2:3, :] * y) * lng_ref[...] + lnb_ref[...]
    x1_ref[...] = x1
    h = _ln(x1) * (1.0 + m_ref[4:5, :]) + m_ref[3:4, :]
    bucket, g_lo, g_hi = _router(h, rwt_ref, rbias_ref)
    onehot = lax.broadcasted_iota(jnp.int32, (N_BUCKET_PAD, tm), 0) == bucket
    onehot_f = jnp.where(onehot, 1.0, 0.0)
    csum = jnp.dot(onehot_f.astype(BF16), tri_ref[...], preferred_element_type=F32)
    run = run_ref[...]
    rank = jnp.sum(jnp.where(onehot, csum - 1.0 + run[:, 0:1], 0.0), axis=0, keepdims=True)
    run = run + jnp.sum(onehot_f, axis=1, keepdims=True)
    run_ref[...] = run
    cnt_ref[...] = run
    rows8 = lax.broadcasted_iota(jnp.int32, (SUBLANES, tm), 0)
    meta = jnp.where(rows8 == 0, bucket.astype(F32), jnp.where(rows8 == 1, rank, 0.0))
    meta_ref[...] = meta
    gate_rows = jnp.where(rows8 == 0, g_lo, jnp.where(rows8 == 1, g_hi, 0.0))
    gate_cols = jnp.concatenate([gate_rows, jnp.zeros((LANES - SUBLANES, tm), F32)], axis=0).T
    hx_ref[:, :D_MODEL] = h
    hx_ref[:, D_MODEL:] = gate_cols


def _outproj(att, g, gb, x, mod_l, wo, cw, cb, lng, lnb, rwt, rbias, tri, *, tm, cond_row):
    b, n, d = x.shape
    n_tiles = n // tm
    g8 = g.reshape(b, n // SUBLANES, SUBLANES, D_CONV)
    tpb = tm // SUBLANES
    row = lambda w: pl.BlockSpec((None, tm, w), lambda bi, i: (bi, i, 0))
    full = lambda s: pl.BlockSpec(s, lambda bi, i: (0,) * len(s))
    return pl.pallas_call(
        functools.partial(_outproj_kernel, n_tiles=n_tiles),
        grid=(b, n_tiles),
        in_specs=[
            row(D_ATT), row(D_CONV),
            pl.BlockSpec((None, None, SUBLANES, D_CONV), lambda bi, i: (bi, jnp.maximum(i * tpb - 1, 0), 0, 0)),
            pl.BlockSpec((None, None, SUBLANES, D_CONV),
                         lambda bi, i: (bi, jnp.minimum((i + 1) * tpb, n // SUBLANES - 1), 0, 0)),
            row(D_CONV), row(d),
            pl.BlockSpec((None, N_MOD_PAD, d), lambda bi, i: (cond_row(bi), 0, 0)),
            full((d, d)), full((3, D_CONV)), full((1, D_CONV)), full((1, d)), full((1, d)),
            full((N_EXPERTS, d)), full((N_EXPERTS, 1)), full((tm, tm)),
        ],
        out_specs=[row(d), row(HX_W), pl.BlockSpec((None, SUBLANES, tm), lambda bi, i: (bi, 0, i)),
                   full((N_BUCKET_PAD, LANES))],
        out_shape=[jax.ShapeDtypeStruct((b, n, d), F32), jax.ShapeDtypeStruct((b, n, HX_W), F32),
                   jax.ShapeDtypeStruct((b, SUBLANES, n), F32),
                   jax.ShapeDtypeStruct((N_BUCKET_PAD, LANES), F32)],
        scratch_shapes=[pltpu.VMEM((N_BUCKET_PAD, LANES), F32)],
        compiler_params=_params("arbitrary", "arbitrary"),
        name="outproj_ln1_router",
    )(att, g, g8, g8, gb, x, mod_l, wo, cw, cb, lng, lnb, rwt, rbias, tri)


def _dispatch_kernel(pad_start_ref, pad_len_ref, pos_ref, hx_ref, xs_hbm, zero_ref, sem, zero_sem):
    step = pl.program_id(0)
    chunk = hx_ref.shape[0]

    def issue(i, carry):
        pltpu.make_async_copy(hx_ref.at[pl.ds(i, 1)], xs_hbm.at[pl.ds(pos_ref[0, i], 1)], sem).start()
        return carry

    lax.fori_loop(0, chunk, issue, 0, unroll=8)

    @pl.when(step == 0)
    def _():
        zero_ref[...] = jnp.zeros_like(zero_ref)

        half = zero_ref.shape[0]
        n_tiles = xs_hbm.shape[0] // (2 * half)
        n_live = pad_start_ref[N_BUCKETS]

        def pad_copies(act):
            zero_copy = lambda start, rows: act(pltpu.make_async_copy(
                zero_ref.at[pl.ds(0, rows)], xs_hbm.at[pl.ds(start, rows)], zero_sem))
            for b in range(N_BUCKETS):
                start, length = pad_start_ref[b], pad_len_ref[b]
                head = jnp.minimum((-start) & (SUBLANES - 1), length)
                for k in range(SUBLANES - 1):
                    @pl.when(k < head)
                    def _(start=start, k=k):
                        zero_copy(start + k, 1)
                start = start + head
                length = length - head
                piece = half
                while piece >= SUBLANES:
                    @pl.when((length & piece) != 0)
                    def _(start=start, piece=piece):
                        zero_copy(pl.multiple_of(start, SUBLANES), piece)
                    start = start + (length & piece)
                    piece //= 2
            for k in range(N_BUCKETS):
                @pl.when(n_tiles - 1 - k >= n_live)
                def _(k=k):
                    zero_copy((n_tiles - 1 - k) * 2 * half, half)
                    zero_copy((n_tiles - 1 - k) * 2 * half + half, half)

        pad_copies(lambda copy: copy.start())
        pad_copies(lambda copy: copy.wait())

    pltpu.make_async_copy(hx_ref, xs_hbm.at[pl.ds(0, chunk)], sem).wait()


def _dispatch(pos, pad_start, pad_len, hx, n_rows, *, tms):
    n, w = hx.shape
    chunk = DISPATCH_CHUNK
    return pl.pallas_call(
        _dispatch_kernel,
        grid_spec=pltpu.PrefetchScalarGridSpec(
            num_scalar_prefetch=2,
            grid=(n // chunk,),
            in_specs=[pl.BlockSpec((None, 1, chunk), lambda i, ps, pn: (i, 0, 0), memory_space=pltpu.SMEM),
                      pl.BlockSpec((chunk, w), lambda i, ps, pn: (i, 0))],
            out_specs=pl.BlockSpec(memory_space=pl.ANY),
            scratch_shapes=[pltpu.VMEM((tms // 2, w), F32), pltpu.SemaphoreType.DMA(()),
                            pltpu.SemaphoreType.DMA(())],
        ),
        out_shape=jax.ShapeDtypeStruct((n_rows, w), F32),
        compiler_params=_params("arbitrary"),
        name="moe_dispatch",
    )(pad_start, pad_len, pos.reshape(n // chunk, 1, chunk), hx)


def _experts_kernel(blk_ref, ea_ref, eb_ref, nv_ref, xs_ref, wga_ref, wua_ref, wda_ref, wgb_ref, wub_ref, wdb_ref,
                    ys_ref):
    nv = nv_ref[pl.program_id(0)]

    @pl.when(nv == 0)
    def _():
        ys_ref[...] = jnp.zeros_like(ys_ref)

    @pl.when(nv > 0)
    def _():
        x = xs_ref[:, :D_MODEL].astype(BF16)
        g_lo = xs_ref[:, D_MODEL:D_MODEL + 1]
        g_hi = xs_ref[:, D_MODEL + 1:D_MODEL + 2]

        def ffn(wg_ref, wu_ref, wd_ref):
            gate = jnp.dot(x, wg_ref[...], preferred_element_type=F32)
            up = jnp.dot(x, wu_ref[...], preferred_element_type=F32)
            he = (gate * _sigmoid(gate) * up).astype(BF16)
            return jnp.dot(he, wd_ref[...], preferred_element_type=F32)

        ys_ref[...] = g_lo * ffn(wga_ref, wua_ref, wda_ref) + g_hi * ffn(wgb_ref, wub_ref, wdb_ref)


def _experts(tile_blk, tile_ea, tile_eb, tile_nv, xs, wg, wu, wd, *, tms):
    n_rows, w = xs.shape
    d = D_MODEL
    n_tiles = n_rows // tms
    wspec = lambda shape, which: pl.BlockSpec(
        (None,) + shape, lambda j, blk, ea, eb, nv: ((ea, eb)[which][j], 0, 0))
    return pl.pallas_call(
        _experts_kernel,
        grid_spec=pltpu.PrefetchScalarGridSpec(
            num_scalar_prefetch=4,
            grid=(n_tiles,),
            in_specs=[
                pl.BlockSpec((tms, w), lambda j, blk, ea, eb, nv: (blk[j], 0)),
                wspec((d, D_EXPERT), 0), wspec((d, D_EXPERT), 0), wspec((D_EXPERT, d), 0),
                wspec((d, D_EXPERT), 1), wspec((d, D_EXPERT), 1), wspec((D_EXPERT, d), 1),
            ],
            out_specs=pl.BlockSpec((tms, d), lambda j, blk, ea, eb, nv: (j, 0)),
        ),
        out_shape=jax.ShapeDtypeStruct((n_rows, d), F32),
        compiler_params=_params("arbitrary"),
        name="moe_experts",
    )(tile_blk, tile_ea, tile_eb, tile_nv, xs, wg, wu, wd, wg, wu, wd)


def _combine_kernel(pos_ref, pos_next_ref, ys_hbm, x1_ref, m_ref, lng_ref, lnb_ref, o_ref, buf_ref, sem):
    step = pl.program_id(0)
    tm = buf_ref.shape[1]

    def gather(p_ref, slot):
        def issue(i, carry):
            pltpu.make_async_copy(ys_hbm.at[pl.ds(p_ref[0, i], 1)], buf_ref.at[slot, pl.ds(i, 1)],
                                  sem.at[slot]).start()
            return carry

        lax.fori_loop(0, tm, issue, 0, unroll=8)

    @pl.when(step == 0)
    def _():
        gather(pos_ref, 0)

    slot = step % 2

    @pl.when(step + 1 < pl.num_programs(0))
    def _():
        gather(pos_next_ref, 1 - slot)

    pltpu.make_async_copy(ys_hbm.at[pl.ds(0, tm)], buf_ref.at[slot], sem.at[slot]).wait()
    o_ref[...] = _ln(ALPHA * x1_ref[...] + m_ref[5:6, :] * buf_ref[slot]) * lng_ref[...] + lnb_ref[...]


def _combine(pos, ys, x1, mod_l, lng, lnb, *, cond_row):
    n, d = x1.shape
    tm = TM_COMBINE
    n_tiles = n // tm
    vec = pl.BlockSpec((1, d), lambda i: (0, 0))
    pos3 = pos.reshape(n_tiles, 1, tm)
    return pl.pallas_call(
        _combine_kernel,
        grid=(n_tiles,),
        in_specs=[
            pl.BlockSpec((None, 1, tm), lambda i: (i, 0, 0), memory_space=pltpu.SMEM),
            pl.BlockSpec((None, 1, tm), lambda i: (jnp.minimum(i + 1, n_tiles - 1), 0, 0),
                         memory_space=pltpu.SMEM),
            pl.BlockSpec(memory_space=pl.ANY),
            pl.BlockSpec((tm, d), lambda i: (i, 0)),
            pl.BlockSpec((None, N_MOD_PAD, d), lambda i: (cond_row(i), 0, 0)),
            vec, vec,
        ],
        out_specs=pl.BlockSpec((tm, d), lambda i: (i, 0)),
        out_shape=jax.ShapeDtypeStruct((n, d), F32),
        scratch_shapes=[pltpu.VMEM((2, tm, d), F32), pltpu.SemaphoreType.DMA((2,))],
        compiler_params=_params("arbitrary"),
        name="moe_combine_ln2",
    )(pos3, pos3, ys, x1, mod_l, lng, lnb)


def _routing_plan(meta, counts, *, tms):
    b, _, n = meta.shape
    n_tok = b * n
    bucket = meta[:, 0, :].reshape(n_tok).astype(jnp.int32)
    rank = meta[:, 1, :].reshape(n_tok).astype(jnp.int32)
    cnt = counts[:N_BUCKETS, 0].astype(jnp.int32)
    tiles = (cnt + tms - 1) // tms
    tile_end = jnp.cumsum(tiles)
    tile_off = tile_end - tiles
    n_tiles = n_tok // tms + N_BUCKETS
    ids = jnp.arange(N_BUCKETS, dtype=jnp.int32)
    pos = rank + jnp.sum(jnp.where(bucket[:, None] == ids[None, :], (tile_off * tms)[None, :], 0), axis=1)
    j = jnp.arange(n_tiles, dtype=jnp.int32)
    live = j < tile_end[-1]
    jb = jnp.minimum(j, tile_end[-1] - 1)
    tb = jnp.sum((jb[:, None] >= tile_end[None, :]).astype(jnp.int32), axis=1)
    pick = lambda table: jnp.sum(jnp.where(tb[:, None] == ids[None, :], table[None, :], 0), axis=1)
    nv = jnp.where(live, jnp.clip(pick(cnt) - (jb - pick(tile_off)) * tms, 0, tms), 0)
    pair = ids % PAIRS_PER_GROUP
    lo = jnp.where(pair < 3, 0, jnp.where(pair < 5, 1, 2))
    hi = jnp.where(pair < 3, pair + 1, jnp.where(pair < 5, pair - 1, 3))
    group = ids // PAIRS_PER_GROUP
    ea = pick(group * EXPERTS_PER_GROUP + lo)
    eb = pick(group * EXPERTS_PER_GROUP + hi)
    pad_start = jnp.concatenate([tile_off * tms + cnt, tile_end[-1:]]).astype(jnp.int32)
    pad_len = jnp.concatenate([tiles * tms - cnt, jnp.zeros((1,), jnp.int32)]).astype(jnp.int32)
    return pos, (pad_start, pad_len), (jb, ea, eb, nv.astype(jnp.int32)), n_tiles * tms


def _moe(hx, meta, counts, x1, moe_w, mod_l, lng, lnb, *, tms, cond_row):
    d = D_MODEL
    pos, pads, tile_tables, n_rows = _routing_plan(meta, counts, tms=tms)
    xs = _dispatch(pos, *pads, hx.reshape(-1, HX_W), n_rows, tms=tms)
    ys = _experts(*tile_tables, xs, *moe_w, tms=tms)
    return _combine(pos, ys, x1.reshape(-1, d), mod_l, lng, lnb, cond_row=cond_row)


def _rope_tables():
    t = jnp.arange(SEQ, dtype=jnp.int32)
    pos = jnp.stack([(t // GRID_W).astype(F32), (t % GRID_W).astype(F32)], axis=1)
    inv = 1.0 / (ROPE_THETA ** (jnp.arange(N_FREQ, dtype=F32) / N_FREQ))
    ang = pos[:, :, None] * inv
    cos, sin = jnp.cos(ang), jnp.sin(ang)
    zero = jnp.zeros_like(sin)
    expand = lambda first, second: jnp.tile(
        jnp.stack([first, second], axis=2).reshape(SEQ, DQK), (1, LANES // DQK))
    return expand(cos, cos), expand(-sin, zero), expand(zero, sin)


def kernel(x, c, ctx, c_ctx, w_mod, b_mod, w_in, diff_lambda, attn_norm_g, conv_w, conv_b, w_out,
           ln1_g, ln1_b, ln2_g, ln2_b, router_w, router_bias, w_gate, w_up, w_down):
    assert x.shape == (BATCH, SEQ, D_MODEL) and ctx.shape == (BATCH, CTX_LEN, D_MODEL)
    d = D_MODEL
    cc = jnp.concatenate([c, c_ctx[None, :], jnp.zeros((N_COND_PAD - N_COND, d), F32)], axis=0)
    mod = _modulation(cc, w_mod, b_mod)
    rope_tabs = _rope_tables()
    w_in_b, w_out_b = w_in.astype(BF16), w_out.astype(BF16)
    wg_b, wu_b, wd_b = w_gate.astype(BF16), w_up.astype(BF16), w_down.astype(BF16)
    rwt = router_w.T
    rbias = router_bias.reshape(N_EXPERTS, 1)

    lat_row = lambda bi: bi
    ctx_row = lambda bi: CTX_ROW
    lat_tile_row = lambda i: i // (SEQ // TM_COMBINE)
    tri = lambda tm: (jnp.arange(tm)[:, None] <= jnp.arange(tm)[None, :]).astype(BF16)

    for l in range(DEPTH):
        last = l == DEPTH - 1
        lam_init = 0.8 - 0.6 * math.exp(-0.3 * l)
        vec = lambda p, w: p[l].reshape(1, w)
        post = (w_out_b[l], conv_w[l], vec(conv_b, D_CONV), vec(ln1_g, d), vec(ln1_b, d), rwt, rbias)
        ln2 = (vec(ln2_g, d), vec(ln2_b, d))

        q, kt, v, g, gb = _inproj(x, mod[l], w_in_b[l], rope_tabs, tm=TM_LAT, cond_row=lat_row)
        if last:
            ktc, vc = _inproj(ctx, mod[l], w_in_b[l][:, D_ATT:3 * D_ATT], None, tm=TM_CTX, cond_row=ctx_row,
                              kv_only=True)
        else:
            qc, ktc, vc, gc, gbc = _inproj(ctx, mod[l], w_in_b[l], None, tm=TM_CTX, cond_row=ctx_row)

        att = _attention(diff_lambda[l], q, [(kt, v), (ktc, vc)], attn_norm_g[l], lam_init=lam_init)
        x1, hx, meta, counts = _outproj(att, g, gb, x, mod[l], *post, tri(TM_LAT), tm=TM_LAT, cond_row=lat_row)
        moe_w = (wg_b[l], wu_b[l], wd_b[l])
        x = _moe(hx, meta, counts, x1, moe_w, mod[l], *ln2, tms=TMS_LAT, cond_row=lat_tile_row
                 ).reshape(BATCH, SEQ, d)
        if not last:
            att_c = _attention(diff_lambda[l], qc, [(ktc, vc)], attn_norm_g[l], lam_init=lam_init)
            c1, hxc, meta_c, counts_c = _outproj(att_c, gc, gbc, ctx, mod[l], *post, tri(TM_CTX), tm=TM_CTX,
                                                 cond_row=ctx_row)
            ctx = _moe(hxc, meta_c, counts_c, c1, moe_w, mod[l], *ln2, tms=TMS_CTX, cond_row=ctx_row
                       ).reshape(BATCH, CTX_LEN, d)
    return x
```

```python
import functools
import math

import jax
import jax.numpy as jnp
from jax import lax
from jax.experimental import pallas as pl
from jax.experimental.pallas import tpu as pltpu

D_MODEL = 1024
BATCH = 16
SEQ = 4096
DEPTH = 2
GRID_W = 64
CTX_LEN = 256
D_ATT = D_MODEL // 2
D_CONV = D_MODEL - D_ATT
N_HEADS = 4
DV = D_ATT // N_HEADS
DQK = DV // 2
N_FREQ = DQK // 4
ROPE_THETA = 10000.0
N_EXPERTS = 16
N_GROUPS = 4
EXPERTS_PER_GROUP = N_EXPERTS // N_GROUPS
D_EXPERT = 512
LN_EPS = 1e-6
HEAD_NORM_EPS = 1e-5
ALPHA = (2 * DEPTH) ** 0.25

F32 = jnp.float32
BF16 = jnp.bfloat16

LANES = 128
SUBLANES = 8
VMEM_LIMIT_BYTES = 48 * 1024 * 1024

N_COND = BATCH + 1
N_COND_PAD = 24
CTX_ROW = BATCH
N_MOD = 6
N_MOD_PAD = SUBLANES

TM_LAT = 512
TM_CTX = CTX_LEN
TQ = 256
N_SUB = 2
MIN_ROW_SUM = 2.0 ** -100
RB = 16
LOG2E = math.log2(math.e)
PAIRS_PER_GROUP = EXPERTS_PER_GROUP * (EXPERTS_PER_GROUP - 1) // 2
N_BUCKETS = N_GROUPS * PAIRS_PER_GROUP
N_BUCKET_PAD = 32
HX_W = D_MODEL + LANES
TMS_LAT = 512
TMS_CTX = 256
DISPATCH_CHUNK = 2048
TM_COMBINE = 256


def _params(*sem):
    return pltpu.CompilerParams(dimension_semantics=sem, vmem_limit_bytes=VMEM_LIMIT_BYTES)


def _ln(x):
    mu = jnp.mean(x, axis=-1, keepdims=True)
    xc = x - mu
    var = jnp.mean(xc * xc, axis=-1, keepdims=True)
    return xc * lax.rsqrt(var + LN_EPS)


def _sigmoid(x):
    return 1.0 / (1.0 + jnp.exp(-x))


def _split_bf16(x):
    hi = x.astype(BF16)
    lo = (x - hi.astype(F32)).astype(BF16)
    return hi, lo


def _mod_kernel(cc_ref, w_ref, b_ref, o_ref):
    a = cc_ref[...]
    a = a * _sigmoid(a)
    a_hi, a_lo = _split_bf16(a)
    w_hi, w_lo = _split_bf16(w_ref[...])
    m = (jnp.dot(a_hi, w_hi, preferred_element_type=F32)
         + jnp.dot(a_lo, w_hi, preferred_element_type=F32)
         + jnp.dot(a_hi, w_lo, preferred_element_type=F32))
    o_ref[...] = m + b_ref[...]


def _modulation(cc, w_mod, b_mod):
    d = D_MODEL
    out = pl.pallas_call(
        _mod_kernel,
        grid=(DEPTH, N_MOD),
        in_specs=[
            pl.BlockSpec((N_COND_PAD, d), lambda l, j: (0, 0)),
            pl.BlockSpec((None, d, d), lambda l, j: (l, 0, j)),
            pl.BlockSpec((None, 1, d), lambda l, j: (l, 0, j)),
        ],
        out_specs=pl.BlockSpec((None, N_COND_PAD, d), lambda l, j: (l, 0, j)),
        out_shape=jax.ShapeDtypeStruct((DEPTH, N_COND_PAD, N_MOD * d), F32),
        compiler_params=_params("parallel", "parallel"),
        name="modulation",
    )(cc, w_mod, b_mod.reshape(DEPTH, 1, N_MOD * d))
    m = out.reshape(DEPTH, N_COND_PAD, N_MOD, d)
    return jnp.pad(m, ((0, 0), (0, 0), (0, N_MOD_PAD - N_MOD), (0, 0)))


def _inproj_kernel(*refs, use_rope, kv_only):
    x_ref, m_ref, w_ref = refs[:3]
    rest = refs[3:]
    if use_rope:
        ra_ref, rb_ref, rc_ref = rest[:3]
        rest = rest[3:]
    u = _ln(x_ref[...]) * (1.0 + m_ref[1:2, :]) + m_ref[0:1, :]
    y = jnp.dot(u.astype(BF16), w_ref[...], preferred_element_type=F32)
    if kv_only:
        kt_ref, v_ref = rest
        kt_ref[...] = y[:, :D_ATT].T.astype(BF16)
        v_ref[...] = y[:, D_ATT:2 * D_ATT].astype(BF16)
        return
    q_ref, kt_ref, v_ref, g_ref, gb_ref = rest
    q = y[:, :D_ATT] * (DQK ** -0.5 * LOG2E)
    k = y[:, D_ATT:2 * D_ATT]
    if use_rope:
        ra, rb, rc = ra_ref[...], rb_ref[...], rc_ref[...]

        def rope(t):
            cols = []
            for j in range(D_ATT // LANES):
                tj = t[:, j * LANES:(j + 1) * LANES]
                cols.append(tj * ra
                            + pltpu.roll(tj, LANES - N_FREQ, 1) * rb
                            + pltpu.roll(tj, N_FREQ, 1) * rc)
            return jnp.concatenate(cols, axis=1)

        q = rope(q)
        k = rope(k)
    q_ref[...] = q.astype(BF16)
    kt_ref[...] = k.T.astype(BF16)
    v_ref[...] = y[:, 2 * D_ATT:3 * D_ATT].astype(BF16)
    gb_ref[...] = y[:, 3 * D_ATT:3 * D_ATT + D_CONV]
    g_ref[...] = y[:, 3 * D_ATT + D_CONV:3 * D_ATT + 2 * D_CONV] * y[:, 3 * D_ATT + 2 * D_CONV:]


def _inproj(x, mod_l, w, rope_tabs, *, tm, cond_row, kv_only=False):
    b, n, d = x.shape
    use_rope = rope_tabs is not None
    wn = w.shape[1]
    in_specs = [
        pl.BlockSpec((None, tm, d), lambda bi, i: (bi, i, 0)),
        pl.BlockSpec((None, N_MOD_PAD, d), lambda bi, i: (cond_row(bi), 0, 0)),
        pl.BlockSpec((d, wn), lambda bi, i: (0, 0)),
    ]
    args = [x, mod_l, w]
    if use_rope:
        in_specs += [pl.BlockSpec((tm, LANES), lambda bi, i: (i, 0))] * 3
        args += list(rope_tabs)
    kt_spec = pl.BlockSpec((None, D_ATT, tm), lambda bi, i: (bi, 0, i))
    row_spec = pl.BlockSpec((None, tm, D_ATT), lambda bi, i: (bi, i, 0))
    kt_shape = jax.ShapeDtypeStruct((b, D_ATT, n), BF16)
    if kv_only:
        out_specs = [kt_spec, row_spec]
        out_shape = [kt_shape, jax.ShapeDtypeStruct((b, n, D_ATT), BF16)]
    else:
        out_specs = [row_spec, kt_spec, row_spec, row_spec, row_spec]
        out_shape = [jax.ShapeDtypeStruct((b, n, D_ATT), BF16), kt_shape,
                     jax.ShapeDtypeStruct((b, n, D_ATT), BF16),
                     jax.ShapeDtypeStruct((b, n, D_CONV), F32),
                     jax.ShapeDtypeStruct((b, n, D_CONV), F32)]
    return pl.pallas_call(
        functools.partial(_inproj_kernel, use_rope=use_rope, kv_only=kv_only),
        grid=(b, n // tm),
        in_specs=in_specs,
        out_specs=out_specs,
        out_shape=out_shape,
        compiler_params=_params("parallel", "parallel"),
        name="inproj",
    )(*args)


def _attn_kernel(*refs, n_seg, lam_init):
    lp_ref, q_ref = refs[:2]
    seg_refs = refs[2:2 + 2 * n_seg]
    gn_ref, o_ref, s_ref, a_ref, ml_ref, r_ref, kn_ref = refs[2 + 2 * n_seg:]
    kt_refs = seg_refs[0::2]
    v_refs = seg_refs[1::2]
    n_sub, two_tq, _ = s_ref.shape
    tq = two_tq // 2

    lp = lp_ref[...]
    lam = (jnp.exp(jnp.sum(lp[0:1, :] * lp[1:2, :], axis=-1, keepdims=True))
           - jnp.exp(jnp.sum(lp[2:3, :] * lp[3:4, :], axis=-1, keepdims=True)) + lam_init)

    @pl.when(pl.program_id(2) == 0)
    def _():
        k1, k2 = None, None
        for kt_ref in kt_refs:
            ksq = jnp.square(kt_ref[...].astype(F32))
            m1 = jnp.max(jnp.sum(ksq[:DQK, :], axis=0, keepdims=True), axis=-1, keepdims=True)
            m2 = jnp.max(jnp.sum(ksq[DQK:, :], axis=0, keepdims=True), axis=-1, keepdims=True)
            k1 = m1 if k1 is None else jnp.maximum(k1, m1)
            k2 = m2 if k2 is None else jnp.maximum(k2, m2)
        kn_ref[0:1, :] = jnp.broadcast_to(k1, (1, LANES))
        kn_ref[1:2, :] = jnp.broadcast_to(k2, (1, LANES))

    def stacked_q(j):
        q = q_ref[j * tq:(j + 1) * tq, :]
        lane = lax.broadcasted_iota(jnp.int32, q.shape, 1)
        zero = jnp.zeros_like(q)
        return jnp.concatenate([jnp.where(lane < DQK, q, zero), jnp.where(lane >= DQK, q, zero)], axis=0)

    def scores(j):
        qz = stacked_q(j)
        off = 0
        for kt_ref in kt_refs:
            nk = kt_ref.shape[1]
            s_ref[j, :, off:off + nk] = jnp.dot(qz, kt_ref[...], preferred_element_type=F32)
            off += nk

    nk_total = s_ref.shape[2]
    n_rep = nk_total // LANES

    def softmax_bounded(j):
        qz = stacked_q(j)
        qsq = jnp.sum(jnp.square(qz.astype(F32)), axis=-1, keepdims=True)
        knorm = jnp.concatenate([jnp.broadcast_to(kn_ref[0:1, :], (tq, LANES)),
                                 jnp.broadcast_to(kn_ref[1:2, :], (tq, LANES))], axis=0)
        bound = jnp.sqrt(qsq * knorm)
        total = None
        off = 0
        for kt_ref in kt_refs:
            nk = kt_ref.shape[1]
            e = jnp.exp2(jnp.dot(qz, kt_ref[...], preferred_element_type=F32)
                         - jnp.concatenate([bound] * (nk // LANES), axis=1))
            s_ref[j, :, off:off + nk] = e
            part = jnp.sum(e, axis=-1, keepdims=True)
            total = part if total is None else total + part
            off += nk
        ml_ref[j] = jnp.broadcast_to(total, (two_tq, LANES))
        return jnp.min(total)

    def softmax_exact(j):
        for g in range(two_tq // SUBLANES):
            rows = slice(g * SUBLANES, (g + 1) * SUBLANES)
            m = jnp.max(s_ref[j, rows, :], axis=-1, keepdims=True)
            ml_ref[j, rows, :] = jnp.broadcast_to(m, (SUBLANES, LANES))
        for g in range(two_tq // SUBLANES):
            rows = slice(g * SUBLANES, (g + 1) * SUBLANES)
            e = jnp.exp2(s_ref[j, rows, :] - jnp.concatenate([ml_ref[j, rows, :]] * n_rep, axis=1))
            s_ref[j, rows, :] = e
            ml_ref[j, rows, :] = jnp.broadcast_to(jnp.sum(e, axis=-1, keepdims=True), (SUBLANES, LANES))

    def combine(j):
        for r in range(tq // RB):
            rows1 = slice(r * RB, (r + 1) * RB)
            rows2 = slice(tq + r * RB, tq + (r + 1) * RB)
            l1 = ml_ref[j, rows1, :]
            c = jnp.concatenate([lam * l1 / ml_ref[j, rows2, :]] * n_rep, axis=1)
            a_ref[j, rows1, :] = (s_ref[j, rows1, :] - c * s_ref[j, rows2, :]).astype(BF16)
            r_ref[j, rows1, :] = 1.0 / l1

    def values(j):
        o = None
        off = 0
        for v_ref in v_refs:
            nk = v_ref.shape[0]
            part = jnp.dot(a_ref[j, :, off:off + nk], v_ref[...], preferred_element_type=F32)
            o = part if o is None else o + part
            off += nk
        o = o * r_ref[j]
        y = o * lax.rsqrt(jnp.mean(o * o, axis=-1, keepdims=True) + HEAD_NORM_EPS)
        o_ref[j * tq:(j + 1) * tq, :] = (y * gn_ref[...] * (1.0 - lam_init)).astype(BF16)

    smallest = None
    for j in range(n_sub):
        low = softmax_bounded(j)
        smallest = low if smallest is None else jnp.minimum(smallest, low)

    @pl.when(jnp.logical_not(smallest >= MIN_ROW_SUM))
    def _():
        for j in range(n_sub):
            scores(j)
            softmax_exact(j)

    for j in range(n_sub):
        combine(j)
        values(j)


def _attention(lam_params, q, segs, attn_g, *, lam_init):
    b, n, _ = q.shape
    tq = min(TQ, n)
    n_sub = min(N_SUB, n // tq)
    tstep = n_sub * tq
    nk_total = sum(kt.shape[2] for kt, _ in segs)
    in_specs = [
        pl.BlockSpec((4, DQK), lambda bi, h, i: (0, 0)),
        pl.BlockSpec((None, tstep, DV), lambda bi, h, i: (bi, i, h)),
    ]
    args = [lam_params, q]
    for kt, v in segs:
        nk = kt.shape[2]
        in_specs += [pl.BlockSpec((None, DV, nk), lambda bi, h, i: (bi, h, 0)),
                     pl.BlockSpec((None, nk, DV), lambda bi, h, i: (bi, 0, h))]
        args += [kt, v]
    in_specs.append(pl.BlockSpec((1, DV), lambda bi, h, i: (0, h)))
    args.append(attn_g.reshape(1, D_ATT))
    return pl.pallas_call(
        functools.partial(_attn_kernel, n_seg=len(segs), lam_init=lam_init),
        grid=(b, N_HEADS, n // tstep),
        in_specs=in_specs,
        out_specs=pl.BlockSpec((None, tstep, DV), lambda bi, h, i: (bi, i, h)),
        out_shape=jax.ShapeDtypeStruct((b, n, D_ATT), BF16),
        scratch_shapes=[pltpu.VMEM((n_sub, 2 * tq, nk_total), F32),
                        pltpu.VMEM((n_sub, tq, nk_total), BF16),
                        pltpu.VMEM((n_sub, 2 * tq, LANES), F32),
                        pltpu.VMEM((n_sub, tq, LANES), F32),
                        pltpu.VMEM((SUBLANES, LANES), F32)],
        compiler_params=_params("parallel", "parallel", "arbitrary"),
        name="diff_attention",
    )(*args)


def _top2_sum(a, b, c, d):
    hi1, lo1 = jnp.maximum(a, b), jnp.minimum(a, b)
    hi2, lo2 = jnp.maximum(c, d), jnp.minimum(c, d)
    return jnp.maximum(hi1, hi2) + jnp.maximum(jnp.minimum(hi1, hi2), jnp.maximum(lo1, lo2))


def _router(h, rwt_ref, rbias_ref):
    tm = h.shape[0]
    h_hi, h_lo = _split_bf16(h)
    w_hi, w_lo = _split_bf16(rwt_ref[...])
    nt = (((1,), (1,)), ((), ()))
    logits = (lax.dot_general(w_hi, h_hi, nt, preferred_element_type=F32)
              + lax.dot_general(w_hi, h_lo, nt, preferred_element_type=F32)
              + lax.dot_general(w_lo, h_hi, nt, preferred_element_type=F32))
    scores = _sigmoid(logits)
    sel = scores + rbias_ref[...]
    srow = [sel[e:e + 1, :] for e in range(N_EXPERTS)]
    crow = [scores[e:e + 1, :] for e in range(N_EXPERTS)]
    gscore = [_top2_sum(*srow[EXPERTS_PER_GROUP * g:EXPERTS_PER_GROUP * (g + 1)]) for g in range(N_GROUPS)]
    best = jnp.zeros((1, tm), jnp.int32)
    best_v = gscore[0]
    for g in range(1, N_GROUPS):
        upd = gscore[g] > best_v
        best = jnp.where(upd, g, best)
        best_v = jnp.where(upd, gscore[g], best_v)
    cand, cand_score = [], []
    for j in range(EXPERTS_PER_GROUP):
        cs, cc = srow[j], crow[j]
        for g in range(1, N_GROUPS):
            cs = jnp.where(best == g, srow[EXPERTS_PER_GROUP * g + j], cs)
            cc = jnp.where(best == g, crow[EXPERTS_PER_GROUP * g + j], cc)
        cand.append(cs)
        cand_score.append(cc)
    i0 = jnp.zeros((1, tm), jnp.int32)
    v0, w0 = cand[0], cand_score[0]
    for j in range(1, EXPERTS_PER_GROUP):
        upd = cand[j] > v0
        i0 = jnp.where(upd, j, i0)
        v0 = jnp.where(upd, cand[j], v0)
        w0 = jnp.where(upd, cand_score[j], w0)
    i1 = jnp.zeros((1, tm), jnp.int32)
    v1 = jnp.full((1, tm), -jnp.inf, F32)
    w1 = jnp.zeros((1, tm), F32)
    for j in range(EXPERTS_PER_GROUP):
        upd = jnp.logical_and(i0 != j, cand[j] > v1)
        i1 = jnp.where(upd, j, i1)
        v1 = jnp.where(upd, cand[j], v1)
        w1 = jnp.where(upd, cand_score[j], w1)
    wsum = w0 + w1
    first_lo = i0 < i1
    lo = jnp.where(first_lo, i0, i1)
    hi = jnp.where(first_lo, i1, i0)
    g_lo = jnp.where(first_lo, w0, w1) / wsum
    g_hi = jnp.where(first_lo, w1, w0) / wsum
    pair = jnp.where(lo == 0, hi - 1, jnp.where(lo == 1, hi + 1, PAIRS_PER_GROUP - 1))
    return best * PAIRS_PER_GROUP + pair, g_lo, g_hi


def _outproj_kernel(att_ref, g_ref, gprev_ref, gnext_ref, gb_ref, x_ref, m_ref, wo_ref, cw_ref, cb_ref,
                    lng_ref, lnb_ref, rwt_ref, rbias_ref, tri_ref, x1_ref, hx_ref, meta_ref, cnt_ref, run_ref,
                    *, n_tiles):
    i = pl.program_id(1)

    @pl.when(jnp.logical_and(pl.program_id(0) == 0, i == 0))
    def _():
        run_ref[...] = jnp.zeros_like(run_ref)

    g = g_ref[...]
    tm = g.shape[0]
    rows = lax.broadcasted_iota(jnp.int32, g.shape, 0)
    prev_row = jnp.where(i > 0, gprev_ref[SUBLANES - 1:SUBLANES, :], 0.0)
    next_row = jnp.where(i < n_tiles - 1, gnext_ref[0:1, :], 0.0)
    g_m1 = jnp.where(rows == 0, prev_row, pltpu.roll(g, 1, 0))
    g_p1 = jnp.where(rows == tm - 1, next_row, pltpu.roll(g, tm - 1, 0))
    conv = cw_ref[0:1, :] * g_m1 + cw_ref[1:2, :] * g + cw_ref[2:3, :] * g_p1 + cb_ref[...]
    yc = (gb_ref[...] * conv).astype(BF16)
    y = (jnp.dot(att_ref[...], wo_ref[:D_ATT, :], preferred_element_type=F32)
         + jnp.dot(yc, wo_ref[D_ATT:, :], preferred_element_type=F32))
    x1 = _ln(ALPHA * x_ref[...] + m_ref[2:3, :] * y) * lng_ref[...] + lnb_ref[...]
    x1_ref[...] = x1
    h = _ln(x1) * (1.0 + m_ref[4:5, :]) + m_ref[3:4, :]
    bucket, g_lo, g_hi = _router(h, rwt_ref, rbias_ref)
    onehot = lax.broadcasted_iota(jnp.int32, (N_BUCKET_PAD, tm), 0) == bucket
    onehot_f = jnp.where(onehot, 1.0, 0.0)
    csum = jnp.dot(onehot_f.astype(BF16), tri_ref[...], preferred_element_type=F32)
    run = run_ref[...]
    rank = jnp.sum(jnp.where(onehot, csum - 1.0 + run[:, 0:1], 0.0), axis=0, keepdims=True)
    run = run + jnp.sum(onehot_f, axis=1, keepdims=True)
    run_ref[...] = run
    cnt_ref[...] = run
    rows8 = lax.broadcasted_iota(jnp.int32, (SUBLANES, tm), 0)
    meta = jnp.where(rows8 == 0, bucket.astype(F32), jnp.where(rows8 == 1, rank, 0.0))
    meta_ref[...] = meta
    gate_rows = jnp.where(rows8 == 0, g_lo, jnp.where(rows8 == 1, g_hi, 0.0))
    gate_cols = jnp.concatenate([gate_rows, jnp.zeros((LANES - SUBLANES, tm), F32)], axis=0).T
    hx_ref[:, :D_MODEL] = h
    hx_ref[:, D_MODEL:] = gate_cols


def _outproj(att, g, gb, x, mod_l, wo, cw, cb, lng, lnb, rwt, rbias, tri, *, tm, cond_row):
    b, n, d = x.shape
    n_tiles = n // tm
    g8 = g.reshape(b, n // SUBLANES, SUBLANES, D_CONV)
    tpb = tm // SUBLANES
    row = lambda w: pl.BlockSpec((None, tm, w), lambda bi, i: (bi, i, 0))
    full = lambda s: pl.BlockSpec(s, lambda bi, i: (0,) * len(s))
    return pl.pallas_call(
        functools.partial(_outproj_kernel, n_tiles=n_tiles),
        grid=(b, n_tiles),
        in_specs=[
            row(D_ATT), row(D_CONV),
            pl.BlockSpec((None, None, SUBLANES, D_CONV), lambda bi, i: (bi, jnp.maximum(i * tpb - 1, 0), 0, 0)),
            pl.BlockSpec((None, None, SUBLANES, D_CONV),
                         lambda bi, i: (bi, jnp.minimum((i + 1) * tpb, n // SUBLANES - 1), 0, 0)),
            row(D_CONV), row(d),
            pl.BlockSpec((None, N_MOD_PAD, d), lambda bi, i: (cond_row(bi), 0, 0)),
            full((d, d)), full((3, D_CONV)), full((1, D_CONV)), full((1, d)), full((1, d)),
            full((N_EXPERTS, d)), full((N_EXPERTS, 1)), full((tm, tm)),
        ],
        out_specs=[row(d), row(HX_W), pl.BlockSpec((None, SUBLANES, tm), lambda bi, i: (bi, 0, i)),
                   full((N_BUCKET_PAD, LANES))],
        out_shape=[jax.ShapeDtypeStruct((b, n, d), F32), jax.ShapeDtypeStruct((b, n, HX_W), F32),
                   jax.ShapeDtypeStruct((b, SUBLANES, n), F32),
                   jax.ShapeDtypeStruct((N_BUCKET_PAD, LANES), F32)],
        scratch_shapes=[pltpu.VMEM((N_BUCKET_PAD, LANES), F32)],
        compiler_params=_params("arbitrary", "arbitrary"),
        name="outproj_ln1_router",
    )(att, g, g8, g8, gb, x, mod_l, wo, cw, cb, lng, lnb, rwt, rbias, tri)


def _dispatch_kernel(pad_start_ref, pad_len_ref, pos_ref, hx_ref, xs_hbm, zero_ref, sem, zero_sem):
    step = pl.program_id(0)
    chunk = hx_ref.shape[0]

    def issue(g, carry):
        base = pl.multiple_of(g * SUBLANES, SUBLANES)
        for u in range(SUBLANES):
            pltpu.make_async_copy(hx_ref.at[pl.ds(base + u, 1)], xs_hbm.at[pl.ds(pos_ref[0, base + u], 1)],
                                  sem).start()
        return carry

    lax.fori_loop(0, chunk // SUBLANES, issue, 0, unroll=2)

    @pl.when(step == 0)
    def _():
        zero_ref[...] = jnp.zeros_like(zero_ref)

        half = zero_ref.shape[0]
        n_tiles = xs_hbm.shape[0] // (2 * half)
        n_live = pad_start_ref[N_BUCKETS]

        def pad_copies(act):
            zero_copy = lambda start, rows: act(pltpu.make_async_copy(
                zero_ref.at[pl.ds(0, rows)], xs_hbm.at[pl.ds(start, rows)], zero_sem))
            for b in range(N_BUCKETS):
                start, length = pad_start_ref[b], pad_len_ref[b]
                head = jnp.minimum((-start) & (SUBLANES - 1), length)
                for k in range(SUBLANES - 1):
                    @pl.when(k < head)
                    def _(start=start, k=k):
                        zero_copy(start + k, 1)
                start = start + head
                length = length - head
                piece = half
                while piece >= SUBLANES:
                    @pl.when((length & piece) != 0)
                    def _(start=start, piece=piece):
                        zero_copy(pl.multiple_of(start, SUBLANES), piece)
                    start = start + (length & piece)
                    piece //= 2
            for k in range(N_BUCKETS):
                @pl.when(n_tiles - 1 - k >= n_live)
                def _(k=k):
                    zero_copy((n_tiles - 1 - k) * 2 * half, half)
                    zero_copy((n_tiles - 1 - k) * 2 * half + half, half)

        pad_copies(lambda copy: copy.start())
        pad_copies(lambda copy: copy.wait())

    pltpu.make_async_copy(hx_ref, xs_hbm.at[pl.ds(0, chunk)], sem).wait()


def _dispatch(pos, pad_start, pad_len, hx, n_rows, *, tms):
    n, w = hx.shape
    chunk = DISPATCH_CHUNK
    return pl.pallas_call(
        _dispatch_kernel,
        grid_spec=pltpu.PrefetchScalarGridSpec(
            num_scalar_prefetch=2,
            grid=(n // chunk,),
            in_specs=[pl.BlockSpec((None, 1, chunk), lambda i, ps, pn: (i, 0, 0), memory_space=pltpu.SMEM),
                      pl.BlockSpec((chunk, w), lambda i, ps, pn: (i, 0))],
            out_specs=pl.BlockSpec(memory_space=pl.ANY),
            scratch_shapes=[pltpu.VMEM((tms // 2, w), F32), pltpu.SemaphoreType.DMA(()),
                            pltpu.SemaphoreType.DMA(())],
        ),
        out_shape=jax.ShapeDtypeStruct((n_rows, w), F32),
        compiler_params=_params("arbitrary"),
        name="moe_dispatch",
    )(pad_start, pad_len, pos.reshape(n // chunk, 1, chunk), hx)


def _experts_kernel(blk_ref, ea_ref, eb_ref, nv_ref, xs_ref, wga_ref, wua_ref, wda_ref, wgb_ref, wub_ref, wdb_ref,
                    ys_ref):
    nv = nv_ref[pl.program_id(0)]

    @pl.when(nv == 0)
    def _():
        ys_ref[...] = jnp.zeros_like(ys_ref)

    @pl.when(nv > 0)
    def _():
        x = xs_ref[:, :D_MODEL].astype(BF16)
        g_lo = xs_ref[:, D_MODEL:D_MODEL + 1]
        g_hi = xs_ref[:, D_MODEL + 1:D_MODEL + 2]

        def ffn(wg_ref, wu_ref, wd_ref):
            gate = jnp.dot(x, wg_ref[...], preferred_element_type=F32)
            up = jnp.dot(x, wu_ref[...], preferred_element_type=F32)
            he = (gate * _sigmoid(gate) * up).astype(BF16)
            return jnp.dot(he, wd_ref[...], preferred_element_type=F32)

        ys_ref[...] = g_lo * ffn(wga_ref, wua_ref, wda_ref) + g_hi * ffn(wgb_ref, wub_ref, wdb_ref)


def _experts(tile_blk, tile_ea, tile_eb, tile_nv, xs, wg, wu, wd, *, tms):
    n_rows, w = xs.shape
    d = D_MODEL
    n_tiles = n_rows // tms
    wspec = lambda shape, which: pl.BlockSpec(
        (None,) + shape, lambda j, blk, ea, eb, nv: ((ea, eb)[which][j], 0, 0))
    return pl.pallas_call(
        _experts_kernel,
        grid_spec=pltpu.PrefetchScalarGridSpec(
            num_scalar_prefetch=4,
            grid=(n_tiles,),
            in_specs=[
                pl.BlockSpec((tms, w), lambda j, blk, ea, eb, nv: (blk[j], 0)),
                wspec((d, D_EXPERT), 0), wspec((d, D_EXPERT), 0), wspec((D_EXPERT, d), 0),
                wspec((d, D_EXPERT), 1), wspec((d, D_EXPERT), 1), wspec((D_EXPERT, d), 1),
            ],
            out_specs=pl.BlockSpec((tms, d), lambda j, blk, ea, eb, nv: (j, 0)),
        ),
        out_shape=jax.ShapeDtypeStruct((n_rows, d), F32),
        compiler_params=_params("arbitrary"),
        name="moe_experts",
    )(tile_blk, tile_ea, tile_eb, tile_nv, xs, wg, wu, wd, wg, wu, wd)


def _combine_kernel(pos_ref, pos_next_ref, ys_hbm, x1_ref, m_ref, lng_ref, lnb_ref, o_ref, buf_ref, sem):
    step = pl.program_id(0)
    tm = buf_ref.shape[1]

    def gather(p_ref, slot):
        def issue(g, carry):
            base = pl.multiple_of(g * SUBLANES, SUBLANES)
            for u in range(SUBLANES):
                pltpu.make_async_copy(ys_hbm.at[pl.ds(p_ref[0, base + u], 1)],
                                      buf_ref.at[slot, pl.ds(base + u, 1)], sem.at[slot]).start()
            return carry

        lax.fori_loop(0, tm // SUBLANES, issue, 0, unroll=2)

    @pl.when(step == 0)
    def _():
        gather(pos_ref, 0)

    slot = step % 2

    @pl.when(step + 1 < pl.num_programs(0))
    def _():
        gather(pos_next_ref, 1 - slot)

    pltpu.make_async_copy(ys_hbm.at[pl.ds(0, tm)], buf_ref.at[slot], sem.at[slot]).wait()
    o_ref[...] = _ln(ALPHA * x1_ref[...] + m_ref[5:6, :] * buf_ref[slot]) * lng_ref[...] + lnb_ref[...]


def _combine(pos, ys, x1, mod_l, lng, lnb, *, cond_row):
    n, d = x1.shape
    tm = TM_COMBINE
    n_tiles = n // tm
    vec = pl.BlockSpec((1, d), lambda i: (0, 0))
    pos3 = pos.reshape(n_tiles, 1, tm)
    return pl.pallas_call(
        _combine_kernel,
        grid=(n_tiles,),
        in_specs=[
            pl.BlockSpec((None, 1, tm), lambda i: (i, 0, 0), memory_space=pltpu.SMEM),
            pl.BlockSpec((None, 1, tm), lambda i: (jnp.minimum(i + 1, n_tiles - 1), 0, 0),
                         memory_space=pltpu.SMEM),
            pl.BlockSpec(memory_space=pl.ANY),
            pl.BlockSpec((tm, d), lambda i: (i, 0)),
            pl.BlockSpec((None, N_MOD_PAD, d), lambda i: (cond_row(i), 0, 0)),
            vec, vec,
        ],
        out_specs=pl.BlockSpec((tm, d), lambda i: (i, 0)),
        out_shape=jax.ShapeDtypeStruct((n, d), F32),
        scratch_shapes=[pltpu.VMEM((2, tm, d), F32), pltpu.SemaphoreType.DMA((2,))],
        compiler_params=_params("arbitrary"),
        name="moe_combine_ln2",
    )(pos3, pos3, ys, x1, mod_l, lng, lnb)


def _routing_plan(meta, counts, *, tms):
    b, _, n = meta.shape
    n_tok = b * n
    bucket = meta[:, 0, :].reshape(n_tok).astype(jnp.int32)
    rank = meta[:, 1, :].reshape(n_tok).astype(jnp.int32)
    cnt = counts[:N_BUCKETS, 0].astype(jnp.int32)
    tiles = (cnt + tms - 1) // tms
    tile_end = jnp.cumsum(tiles)
    tile_off = tile_end - tiles
    n_tiles = n_tok // tms + N_BUCKETS
    ids = jnp.arange(N_BUCKETS, dtype=jnp.int32)
    pos = rank + jnp.sum(jnp.where(bucket[:, None] == ids[None, :], (tile_off * tms)[None, :], 0), axis=1)
    j = jnp.arange(n_tiles, dtype=jnp.int32)
    live = j < tile_end[-1]
    jb = jnp.minimum(j, tile_end[-1] - 1)
    tb = jnp.sum((jb[:, None] >= tile_end[None, :]).astype(jnp.int32), axis=1)
    pick = lambda table: jnp.sum(jnp.where(tb[:, None] == ids[None, :], table[None, :], 0), axis=1)
    nv = jnp.where(live, jnp.clip(pick(cnt) - (jb - pick(tile_off)) * tms, 0, tms), 0)
    pair = ids % PAIRS_PER_GROUP
    lo = jnp.where(pair < 3, 0, jnp.where(pair < 5, 1, 2))
    hi = jnp.where(pair < 3, pair + 1, jnp.where(pair < 5, pair - 1, 3))
    group = ids // PAIRS_PER_GROUP
    ea = pick(group * EXPERTS_PER_GROUP + lo)
    eb = pick(group * EXPERTS_PER_GROUP + hi)
    pad_start = jnp.concatenate([tile_off * tms + cnt, tile_end[-1:]]).astype(jnp.int32)
    pad_len = jnp.concatenate([tiles * tms - cnt, jnp.zeros((1,), jnp.int32)]).astype(jnp.int32)
    return pos, (pad_start, pad_len), (jb, ea, eb, nv.astype(jnp.int32)), n_tiles * tms


def _moe(hx, meta, counts, x1, moe_w, mod_l, lng, lnb, *, tms, cond_row):
    d = D_MODEL
    pos, pads, tile_tables, n_rows = _routing_plan(meta, counts, tms=tms)
    xs = _dispatch(pos, *pads, hx.reshape(-1, HX_W), n_rows, tms=tms)
    ys = _experts(*tile_tables, xs, *moe_w, tms=tms)
    return _combine(pos, ys, x1.reshape(-1, d), mod_l, lng, lnb, cond_row=cond_row)


def _rope_tables():
    t = jnp.arange(SEQ, dtype=jnp.int32)
    pos = jnp.stack([(t // GRID_W).astype(F32), (t % GRID_W).astype(F32)], axis=1)
    inv = 1.0 / (ROPE_THETA ** (jnp.arange(N_FREQ, dtype=F32) / N_FREQ))
    ang = pos[:, :, None] * inv
    cos, sin = jnp.cos(ang), jnp.sin(ang)
    zero = jnp.zeros_like(sin)
    expand = lambda first, second: jnp.tile(
        jnp.stack([first, second], axis=2).reshape(SEQ, DQK), (1, LANES // DQK))
    return expand(cos, cos), expand(-sin, zero), expand(zero, sin)


def kernel(x, c, ctx, c_ctx, w_mod, b_mod, w_in, diff_lambda, attn_norm_g, conv_w, conv_b, w_out,
           ln1_g, ln1_b, ln2_g, ln2_b, router_w, router_bias, w_gate, w_up, w_down):
    assert x.shape == (BATCH, SEQ, D_MODEL) and ctx.shape == (BATCH, CTX_LEN, D_MODEL)
    d = D_MODEL
    cc = jnp.concatenate([c, c_ctx[None, :], jnp.zeros((N_COND_PAD - N_COND, d), F32)], axis=0)
    mod = _modulation(cc, w_mod, b_mod)
    rope_tabs = _rope_tables()
    w_in_b, w_out_b = w_in.astype(BF16), w_out.astype(BF16)
    wg_b, wu_b, wd_b = w_gate.astype(BF16), w_up.astype(BF16), w_down.astype(BF16)
    rwt = router_w.T
    rbias = router_bias.reshape(N_EXPERTS, 1)

    lat_row = lambda bi: bi
    ctx_row = lambda bi: CTX_ROW
    lat_tile_row = lambda i: i // (SEQ // TM_COMBINE)
    tri = lambda tm: (jnp.arange(tm)[:, None] <= jnp.arange(tm)[None, :]).astype(BF16)

    for l in range(DEPTH):
        last = l == DEPTH - 1
        lam_init = 0.8 - 0.6 * math.exp(-0.3 * l)
        vec = lambda p, w: p[l].reshape(1, w)
        post = (w_out_b[l], conv_w[l], vec(conv_b, D_CONV), vec(ln1_g, d), vec(ln1_b, d), rwt, rbias)
        ln2 = (vec(ln2_g, d), vec(ln2_b, d))

        q, kt, v, g, gb = _inproj(x, mod[l], w_in_b[l], rope_tabs, tm=TM_LAT, cond_row=lat_row)
        if last:
            ktc, vc = _inproj(ctx, mod[l], w_in_b[l][:, D_ATT:3 * D_ATT], None, tm=TM_CTX, cond_row=ctx_row,
                              kv_only=True)
        else:
            qc, ktc, vc, gc, gbc = _inproj(ctx, mod[l], w_in_b[l], None, tm=TM_CTX, cond_row=ctx_row)

        att = _attention(diff_lambda[l], q, [(kt, v), (ktc, vc)], attn_norm_g[l], lam_init=lam_init)
        x1, hx, meta, counts = _outproj(att, g, gb, x, mod[l], *post, tri(TM_LAT), tm=TM_LAT, cond_row=lat_row)
        moe_w = (wg_b[l], wu_b[l], wd_b[l])
        x = _moe(hx, meta, counts, x1, moe_w, mod[l], *ln2, tms=TMS_LAT, cond_row=lat_tile_row
                 ).reshape(BATCH, SEQ, d)
        if not last:
            att_c = _attention(diff_lambda[l], qc, [(ktc, vc)], attn_norm_g[l], lam_init=lam_init)
            c1, hxc, meta_c, counts_c = _outproj(att_c, gc, gbc, ctx, mod[l], *post, tri(TM_CTX), tm=TM_CTX,
                                                 cond_row=ctx_row)
            ctx = _moe(hxc, meta_c, counts_c, c1, moe_w, mod[l], *ln2, tms=TMS_CTX, cond_row=ctx_row
                       ).reshape(BATCH, CTX_LEN, d)
    return x
```

```python
import functools
import math

import jax
import jax.numpy as jnp
from jax import lax
from jax.experimental import pallas as pl
from jax.experimental.pallas import tpu as pltpu

D_MODEL = 1024
BATCH = 16
SEQ = 4096
DEPTH = 2
GRID_W = 64
CTX_LEN = 256
D_ATT = D_MODEL // 2
D_CONV = D_MODEL - D_ATT
N_HEADS = 4
DV = D_ATT // N_HEADS
DQK = DV // 2
N_FREQ = DQK // 4
ROPE_THETA = 10000.0
N_EXPERTS = 16
N_GROUPS = 4
EXPERTS_PER_GROUP = N_EXPERTS // N_GROUPS
D_EXPERT = 512
LN_EPS = 1e-6
HEAD_NORM_EPS = 1e-5
ALPHA = (2 * DEPTH) ** 0.25

F32 = jnp.float32
BF16 = jnp.bfloat16

LANES = 128
SUBLANES = 8
VMEM_LIMIT_BYTES = 58 * 1024 * 1024

N_COND = BATCH + 1
N_COND_PAD = 24
CTX_ROW = BATCH
N_MOD = 6
N_MOD_PAD = SUBLANES

TM_LAT = 512
TM_CTX = CTX_LEN
TQ = 256
N_SUB = 4
MIN_ROW_SUM = 2.0 ** -100
RB = 16
LOG2E = math.log2(math.e)
PAIRS_PER_GROUP = EXPERTS_PER_GROUP * (EXPERTS_PER_GROUP - 1) // 2
N_BUCKETS = N_GROUPS * PAIRS_PER_GROUP
N_BUCKET_PAD = 32
HX_W = D_MODEL + LANES
TMS_LAT = 512
TMS_CTX = 256
DISPATCH_CHUNK = 2048
TM_COMBINE = 512


def _params(*sem):
    return pltpu.CompilerParams(dimension_semantics=sem, vmem_limit_bytes=VMEM_LIMIT_BYTES)


def _ln(x):
    mu = jnp.mean(x, axis=-1, keepdims=True)
    xc = x - mu
    var = jnp.mean(xc * xc, axis=-1, keepdims=True)
    return xc * lax.rsqrt(var + LN_EPS)


def _sigmoid(x):
    return 1.0 / (1.0 + jnp.exp(-x))


def _split_bf16(x):
    hi = x.astype(BF16)
    lo = (x - hi.astype(F32)).astype(BF16)
    return hi, lo


def _mod_kernel(cc_ref, w_ref, b_ref, o_ref):
    a = cc_ref[...]
    a = a * _sigmoid(a)
    a_hi, a_lo = _split_bf16(a)
    w_hi, w_lo = _split_bf16(w_ref[...])
    m = (jnp.dot(a_hi, w_hi, preferred_element_type=F32)
         + jnp.dot(a_lo, w_hi, preferred_element_type=F32)
         + jnp.dot(a_hi, w_lo, preferred_element_type=F32))
    o_ref[...] = m + b_ref[...]


def _modulation(cc, w_mod, b_mod):
    d = D_MODEL
    out = pl.pallas_call(
        _mod_kernel,
        grid=(DEPTH, N_MOD),
        in_specs=[
            pl.BlockSpec((N_COND_PAD, d), lambda l, j: (0, 0)),
            pl.BlockSpec((None, d, d), lambda l, j: (l, 0, j)),
            pl.BlockSpec((None, 1, d), lambda l, j: (l, 0, j)),
        ],
        out_specs=pl.BlockSpec((None, N_COND_PAD, d), lambda l, j: (l, 0, j)),
        out_shape=jax.ShapeDtypeStruct((DEPTH, N_COND_PAD, N_MOD * d), F32),
        compiler_params=_params("parallel", "parallel"),
        name="modulation",
    )(cc, w_mod, b_mod.reshape(DEPTH, 1, N_MOD * d))
    m = out.reshape(DEPTH, N_COND_PAD, N_MOD, d)
    return jnp.pad(m, ((0, 0), (0, 0), (0, N_MOD_PAD - N_MOD), (0, 0)))


def _inproj_kernel(*refs, use_rope, kv_only):
    x_ref, m_ref, w_ref = refs[:3]
    rest = refs[3:]
    if use_rope:
        ra_ref, rb_ref, rc_ref = rest[:3]
        rest = rest[3:]
    u = _ln(x_ref[...]) * (1.0 + m_ref[1:2, :]) + m_ref[0:1, :]
    y = jnp.dot(u.astype(BF16), w_ref[...], preferred_element_type=F32)
    if kv_only:
        kt_ref, v_ref = rest
        kt_ref[...] = y[:, :D_ATT].T.astype(BF16)
        v_ref[...] = y[:, D_ATT:2 * D_ATT].astype(BF16)
        return
    q_ref, kt_ref, v_ref, g_ref, gb_ref = rest
    q = y[:, :D_ATT] * (DQK ** -0.5 * LOG2E)
    k = y[:, D_ATT:2 * D_ATT]
    if use_rope:
        ra, rb, rc = ra_ref[...], rb_ref[...], rc_ref[...]

        def rope(t):
            cols = []
            for j in range(D_ATT // LANES):
                tj = t[:, j * LANES:(j + 1) * LANES]
                cols.append(tj * ra
                            + pltpu.roll(tj, LANES - N_FREQ, 1) * rb
                            + pltpu.roll(tj, N_FREQ, 1) * rc)
            return jnp.concatenate(cols, axis=1)

        q = rope(q)
        k = rope(k)
    q_ref[...] = q.astype(BF16)
    kt_ref[...] = k.T.astype(BF16)
    v_ref[...] = y[:, 2 * D_ATT:3 * D_ATT].astype(BF16)
    gb_ref[...] = y[:, 3 * D_ATT:3 * D_ATT + D_CONV]
    g_ref[...] = y[:, 3 * D_ATT + D_CONV:3 * D_ATT + 2 * D_CONV] * y[:, 3 * D_ATT + 2 * D_CONV:]


def _inproj(x, mod_l, w, rope_tabs, *, tm, cond_row, kv_only=False):
    b, n, d = x.shape
    use_rope = rope_tabs is not None
    wn = w.shape[1]
    in_specs = [
        pl.BlockSpec((None, tm, d), lambda bi, i: (bi, i, 0)),
        pl.BlockSpec((None, N_MOD_PAD, d), lambda bi, i: (cond_row(bi), 0, 0)),
        pl.BlockSpec((d, wn), lambda bi, i: (0, 0)),
    ]
    args = [x, mod_l, w]
    if use_rope:
        in_specs += [pl.BlockSpec((tm, LANES), lambda bi, i: (i, 0))] * 3
        args += list(rope_tabs)
    kt_spec = pl.BlockSpec((None, D_ATT, tm), lambda bi, i: (bi, 0, i))
    row_spec = pl.BlockSpec((None, tm, D_ATT), lambda bi, i: (bi, i, 0))
    kt_shape = jax.ShapeDtypeStruct((b, D_ATT, n), BF16)
    if kv_only:
        out_specs = [kt_spec, row_spec]
        out_shape = [kt_shape, jax.ShapeDtypeStruct((b, n, D_ATT), BF16)]
    else:
        out_specs = [row_spec, kt_spec, row_spec, row_spec, row_spec]
        out_shape = [jax.ShapeDtypeStruct((b, n, D_ATT), BF16), kt_shape,
                     jax.ShapeDtypeStruct((b, n, D_ATT), BF16),
                     jax.ShapeDtypeStruct((b, n, D_CONV), F32),
                     jax.ShapeDtypeStruct((b, n, D_CONV), F32)]
    return pl.pallas_call(
        functools.partial(_inproj_kernel, use_rope=use_rope, kv_only=kv_only),
        grid=(b, n // tm),
        in_specs=in_specs,
        out_specs=out_specs,
        out_shape=out_shape,
        compiler_params=_params("parallel", "parallel"),
        name="inproj",
    )(*args)


def _attn_kernel(*refs, n_seg, lam_init):
    lp_ref, q_ref = refs[:2]
    seg_refs = refs[2:2 + 2 * n_seg]
    gn_ref, o_ref, s_ref, a_ref, ml_ref, r_ref, kn_ref = refs[2 + 2 * n_seg:]
    kt_refs = seg_refs[0::2]
    v_refs = seg_refs[1::2]
    n_sub, two_tq, _ = s_ref.shape
    tq = two_tq // 2

    lp = lp_ref[...]
    lam = (jnp.exp(jnp.sum(lp[0:1, :] * lp[1:2, :], axis=-1, keepdims=True))
           - jnp.exp(jnp.sum(lp[2:3, :] * lp[3:4, :], axis=-1, keepdims=True)) + lam_init)

    @pl.when(pl.program_id(2) == 0)
    def _():
        k1, k2 = None, None
        for kt_ref in kt_refs:
            ksq = jnp.square(kt_ref[...].astype(F32))
            m1 = jnp.max(jnp.sum(ksq[:DQK, :], axis=0, keepdims=True), axis=-1, keepdims=True)
            m2 = jnp.max(jnp.sum(ksq[DQK:, :], axis=0, keepdims=True), axis=-1, keepdims=True)
            k1 = m1 if k1 is None else jnp.maximum(k1, m1)
            k2 = m2 if k2 is None else jnp.maximum(k2, m2)
        kn_ref[0:1, :] = jnp.broadcast_to(k1, (1, LANES))
        kn_ref[1:2, :] = jnp.broadcast_to(k2, (1, LANES))

    def stacked_q(j):
        q = q_ref[j * tq:(j + 1) * tq, :]
        lane = lax.broadcasted_iota(jnp.int32, q.shape, 1)
        zero = jnp.zeros_like(q)
        return jnp.concatenate([jnp.where(lane < DQK, q, zero), jnp.where(lane >= DQK, q, zero)], axis=0)

    def scores(j):
        qz = stacked_q(j)
        off = 0
        for kt_ref in kt_refs:
            nk = kt_ref.shape[1]
            s_ref[j, :, off:off + nk] = jnp.dot(qz, kt_ref[...], preferred_element_type=F32)
            off += nk

    nk_total = s_ref.shape[2]
    n_rep = nk_total // LANES

    def softmax_bounded(j):
        qz = stacked_q(j)
        qsq = jnp.sum(jnp.square(qz.astype(F32)), axis=-1, keepdims=True)
        knorm = jnp.concatenate([jnp.broadcast_to(kn_ref[0:1, :], (tq, LANES)),
                                 jnp.broadcast_to(kn_ref[1:2, :], (tq, LANES))], axis=0)
        bound = jnp.sqrt(qsq * knorm)
        total = None
        off = 0
        for kt_ref in kt_refs:
            nk = kt_ref.shape[1]
            e = jnp.exp2(jnp.dot(qz, kt_ref[...], preferred_element_type=F32)
                         - jnp.concatenate([bound] * (nk // LANES), axis=1))
            s_ref[j, :, off:off + nk] = e
            part = jnp.sum(e, axis=-1, keepdims=True)
            total = part if total is None else total + part
            off += nk
        ml_ref[j] = jnp.broadcast_to(total, (two_tq, LANES))
        return jnp.min(total)

    def softmax_exact(j):
        for g in range(two_tq // SUBLANES):
            rows = slice(g * SUBLANES, (g + 1) * SUBLANES)
            m = jnp.max(s_ref[j, rows, :], axis=-1, keepdims=True)
            ml_ref[j, rows, :] = jnp.broadcast_to(m, (SUBLANES, LANES))
        for g in range(two_tq // SUBLANES):
            rows = slice(g * SUBLANES, (g + 1) * SUBLANES)
            e = jnp.exp2(s_ref[j, rows, :] - jnp.concatenate([ml_ref[j, rows, :]] * n_rep, axis=1))
            s_ref[j, rows, :] = e
            ml_ref[j, rows, :] = jnp.broadcast_to(jnp.sum(e, axis=-1, keepdims=True), (SUBLANES, LANES))

    def combine(j):
        for r in range(tq // RB):
            rows1 = slice(r * RB, (r + 1) * RB)
            rows2 = slice(tq + r * RB, tq + (r + 1) * RB)
            l1 = ml_ref[j, rows1, :]
            c = jnp.concatenate([lam * l1 / ml_ref[j, rows2, :]] * n_rep, axis=1)
            a_ref[j, rows1, :] = (s_ref[j, rows1, :] - c * s_ref[j, rows2, :]).astype(BF16)
            r_ref[j, rows1, :] = 1.0 / l1

    def values(j):
        o = None
        off = 0
        for v_ref in v_refs:
            nk = v_ref.shape[0]
            part = jnp.dot(a_ref[j, :, off:off + nk], v_ref[...], preferred_element_type=F32)
            o = part if o is None else o + part
            off += nk
        o = o * r_ref[j]
        y = o * lax.rsqrt(jnp.mean(o * o, axis=-1, keepdims=True) + HEAD_NORM_EPS)
        o_ref[j * tq:(j + 1) * tq, :] = (y * gn_ref[...] * (1.0 - lam_init)).astype(BF16)

    smallest = None
    for j in range(n_sub):
        low = softmax_bounded(j)
        smallest = low if smallest is None else jnp.minimum(smallest, low)

    @pl.when(jnp.logical_not(smallest >= MIN_ROW_SUM))
    def _():
        for j in range(n_sub):
            scores(j)
            softmax_exact(j)

    for j in range(n_sub):
        combine(j)
        values(j)


def _attention(lam_params, q, segs, attn_g, *, lam_init):
    b, n, _ = q.shape
    tq = min(TQ, n)
    n_sub = min(N_SUB, n // tq)
    tstep = n_sub * tq
    nk_total = sum(kt.shape[2] for kt, _ in segs)
    in_specs = [
        pl.BlockSpec((4, DQK), lambda bi, h, i: (0, 0)),
        pl.BlockSpec((None, tstep, DV), lambda bi, h, i: (bi, i, h)),
    ]
    args = [lam_params, q]
    for kt, v in segs:
        nk = kt.shape[2]
        in_specs += [pl.BlockSpec((None, DV, nk), lambda bi, h, i: (bi, h, 0)),
                     pl.BlockSpec((None, nk, DV), lambda bi, h, i: (bi, 0, h))]
        args += [kt, v]
    in_specs.append(pl.BlockSpec((1, DV), lambda bi, h, i: (0, h)))
    args.append(attn_g.reshape(1, D_ATT))
    return pl.pallas_call(
        functools.partial(_attn_kernel, n_seg=len(segs), lam_init=lam_init),
        grid=(b, N_HEADS, n // tstep),
        in_specs=in_specs,
        out_specs=pl.BlockSpec((None, tstep, DV), lambda bi, h, i: (bi, i, h)),
        out_shape=jax.ShapeDtypeStruct((b, n, D_ATT), BF16),
        scratch_shapes=[pltpu.VMEM((n_sub, 2 * tq, nk_total), F32),
                        pltpu.VMEM((n_sub, tq, nk_total), BF16),
                        pltpu.VMEM((n_sub, 2 * tq, LANES), F32),
                        pltpu.VMEM((n_sub, tq, LANES), F32),
                        pltpu.VMEM((SUBLANES, LANES), F32)],
        compiler_params=_params("parallel", "parallel", "arbitrary"),
        name="diff_attention",
    )(*args)


def _top2_sum(a, b, c, d):
    hi1, lo1 = jnp.maximum(a, b), jnp.minimum(a, b)
    hi2, lo2 = jnp.maximum(c, d), jnp.minimum(c, d)
    return jnp.maximum(hi1, hi2) + jnp.maximum(jnp.minimum(hi1, hi2), jnp.maximum(lo1, lo2))


def _router(h, rwt_ref, rbias_ref):
    tm = h.shape[0]
    h_hi, h_lo = _split_bf16(h)
    w_hi, w_lo = _split_bf16(rwt_ref[...])
    nt = (((1,), (1,)), ((), ()))
    logits = (lax.dot_general(w_hi, h_hi, nt, preferred_element_type=F32)
              + lax.dot_general(w_hi, h_lo, nt, preferred_element_type=F32)
              + lax.dot_general(w_lo, h_hi, nt, preferred_element_type=F32))
    scores = _sigmoid(logits)
    sel = scores + rbias_ref[...]
    srow = [sel[e:e + 1, :] for e in range(N_EXPERTS)]
    crow = [scores[e:e + 1, :] for e in range(N_EXPERTS)]
    gscore = [_top2_sum(*srow[EXPERTS_PER_GROUP * g:EXPERTS_PER_GROUP * (g + 1)]) for g in range(N_GROUPS)]
    best = jnp.zeros((1, tm), jnp.int32)
    best_v = gscore[0]
    for g in range(1, N_GROUPS):
        upd = gscore[g] > best_v
        best = jnp.where(upd, g, best)
        best_v = jnp.where(upd, gscore[g], best_v)
    cand, cand_score = [], []
    for j in range(EXPERTS_PER_GROUP):
        cs, cc = srow[j], crow[j]
        for g in range(1, N_GROUPS):
            cs = jnp.where(best == g, srow[EXPERTS_PER_GROUP * g + j], cs)
            cc = jnp.where(best == g, crow[EXPERTS_PER_GROUP * g + j], cc)
        cand.append(cs)
        cand_score.append(cc)
    i0 = jnp.zeros((1, tm), jnp.int32)
    v0, w0 = cand[0], cand_score[0]
    for j in range(1, EXPERTS_PER_GROUP):
        upd = cand[j] > v0
        i0 = jnp.where(upd, j, i0)
        v0 = jnp.where(upd, cand[j], v0)
        w0 = jnp.where(upd, cand_score[j], w0)
    i1 = jnp.zeros((1, tm), jnp.int32)
    v1 = jnp.full((1, tm), -jnp.inf, F32)
    w1 = jnp.zeros((1, tm), F32)
    for j in range(EXPERTS_PER_GROUP):
        upd = jnp.logical_and(i0 != j, cand[j] > v1)
        i1 = jnp.where(upd, j, i1)
        v1 = jnp.where(upd, cand[j], v1)
        w1 = jnp.where(upd, cand_score[j], w1)
    wsum = w0 + w1
    first_lo = i0 < i1
    lo = jnp.where(first_lo, i0, i1)
    hi = jnp.where(first_lo, i1, i0)
    g_lo = jnp.where(first_lo, w0, w1) / wsum
    g_hi = jnp.where(first_lo, w1, w0) / wsum
    pair = jnp.where(lo == 0, hi - 1, jnp.where(lo == 1, hi + 1, PAIRS_PER_GROUP - 1))
    return best * PAIRS_PER_GROUP + pair, g_lo, g_hi


def _outproj_kernel(att_ref, g_ref, gprev_ref, gnext_ref, gb_ref, x_ref, m_ref, wo_ref, cw_ref, cb_ref,
                    lng_ref, lnb_ref, rwt_ref, rbias_ref, tri_ref, x1_ref, hx_ref, meta_ref, cnt_ref, run_ref,
                    *, n_tiles):
    i = pl.program_id(1)

    @pl.when(jnp.logical_and(pl.program_id(0) == 0, i == 0))
    def _():
        run_ref[...] = jnp.zeros_like(run_ref)

    g = g_ref[...]
    tm = g.shape[0]
    rows = lax.broadcasted_iota(jnp.int32, g.shape, 0)
    prev_row = jnp.where(i > 0, gprev_ref[SUBLANES - 1:SUBLANES, :], 0.0)
    next_row = jnp.where(i < n_tiles - 1, gnext_ref[0:1, :], 0.0)
    g_m1 = jnp.where(rows == 0, prev_row, pltpu.roll(g, 1, 0))
    g_p1 = jnp.where(rows == tm - 1, next_row, pltpu.roll(g, tm - 1, 0))
    conv = cw_ref[0:1, :] * g_m1 + cw_ref[1:2, :] * g + cw_ref[2:3, :] * g_p1 + cb_ref[...]
    yc = (gb_ref[...] * conv).astype(BF16)
    y = (jnp.dot(att_ref[...], wo_ref[:D_ATT, :], preferred_element_type=F32)
         + jnp.dot(yc, wo_ref[D_ATT:, :], preferred_element_type=F32))
    x1 = _ln(ALPHA * x_ref[...] + m_ref[2:3, :] * y) * lng_ref[...] + lnb_ref[...]
    x1_ref[...] = x1
    h = _ln(x1) * (1.0 + m_ref[4:5, :]) + m_ref[3:4, :]
    bucket, g_lo, g_hi = _router(h, rwt_ref, rbias_ref)
    onehot = lax.broadcasted_iota(jnp.int32, (N_BUCKET_PAD, tm), 0) == bucket
    onehot_f = jnp.where(onehot, 1.0, 0.0)
    csum = jnp.dot(onehot_f.astype(BF16), tri_ref[...], preferred_element_type=F32)
    run = run_ref[...]
    rank = jnp.sum(jnp.where(onehot, csum - 1.0 + run[:, 0:1], 0.0), axis=0, keepdims=True)
    run = run + jnp.sum(onehot_f, axis=1, keepdims=True)
    run_ref[...] = run
    cnt_ref[...] = run
    rows8 = lax.broadcasted_iota(jnp.int32, (SUBLANES, tm), 0)
    meta = jnp.where(rows8 == 0, bucket.astype(F32), jnp.where(rows8 == 1, rank, 0.0))
    meta_ref[...] = meta
    gate_rows = jnp.where(rows8 == 0, g_lo, jnp.where(rows8 == 1, g_hi, 0.0))
    gate_cols = jnp.concatenate([gate_rows, jnp.zeros((LANES - SUBLANES, tm), F32)], axis=0).T
    hx_ref[:, :D_MODEL] = h
    hx_ref[:, D_MODEL:] = gate_cols


def _outproj(att, g, gb, x, mod_l, wo, cw, cb, lng, lnb, rwt, rbias, tri, *, tm, cond_row):
    b, n, d = x.shape
    n_tiles = n // tm
    g8 = g.reshape(b, n // SUBLANES, SUBLANES, D_CONV)
    tpb = tm // SUBLANES
    row = lambda w: pl.BlockSpec((None, tm, w), lambda bi, i: (bi, i, 0))
    full = lambda s: pl.BlockSpec(s, lambda bi, i: (0,) * len(s))
    return pl.pallas_call(
        functools.partial(_outproj_kernel, n_tiles=n_tiles),
        grid=(b, n_tiles),
        in_specs=[
            row(D_ATT), row(D_CONV),
            pl.BlockSpec((None, None, SUBLANES, D_CONV), lambda bi, i: (bi, jnp.maximum(i * tpb - 1, 0), 0, 0)),
            pl.BlockSpec((None, None, SUBLANES, D_CONV),
                         lambda bi, i: (bi, jnp.minimum((i + 1) * tpb, n // SUBLANES - 1), 0, 0)),
            row(D_CONV), row(d),
            pl.BlockSpec((None, N_MOD_PAD, d), lambda bi, i: (cond_row(bi), 0, 0)),
            full((d, d)), full((3, D_CONV)), full((1, D_CONV)), full((1, d)), full((1, d)),
            full((N_EXPERTS, d)), full((N_EXPERTS, 1)), full((tm, tm)),
        ],
        out_specs=[row(d), row(HX_W), pl.BlockSpec((None, SUBLANES, tm), lambda bi, i: (bi, 0, i)),
                   full((N_BUCKET_PAD, LANES))],
        out_shape=[jax.ShapeDtypeStruct((b, n, d), F32), jax.ShapeDtypeStruct((b, n, HX_W), F32),
                   jax.ShapeDtypeStruct((b, SUBLANES, n), F32),
                   jax.ShapeDtypeStruct((N_BUCKET_PAD, LANES), F32)],
        scratch_shapes=[pltpu.VMEM((N_BUCKET_PAD, LANES), F32)],
        compiler_params=_params("arbitrary", "arbitrary"),
        name="outproj_ln1_router",
    )(att, g, g8, g8, gb, x, mod_l, wo, cw, cb, lng, lnb, rwt, rbias, tri)


def _dispatch_kernel(pad_start_ref, pad_len_ref, pos_ref, hx_ref, xs_hbm, zero_ref, sem, zero_sem):
    step = pl.program_id(0)
    chunk = hx_ref.shape[0]

    def issue(g, carry):
        base = pl.multiple_of(g * SUBLANES, SUBLANES)
        for u in range(SUBLANES):
            pltpu.make_async_copy(hx_ref.at[pl.ds(base + u, 1)], xs_hbm.at[pl.ds(pos_ref[0, base + u], 1)],
                                  sem).start()
        return carry

    lax.fori_loop(0, chunk // SUBLANES, issue, 0, unroll=2)

    @pl.when(step == 0)
    def _():
        zero_ref[...] = jnp.zeros_like(zero_ref)

        half = zero_ref.shape[0]
        n_tiles = xs_hbm.shape[0] // (2 * half)
        n_live = pad_start_ref[N_BUCKETS]

        def pad_copies(act):
            zero_copy = lambda start, rows: act(pltpu.make_async_copy(
                zero_ref.at[pl.ds(0, rows)], xs_hbm.at[pl.ds(start, rows)], zero_sem))
            for b in range(N_BUCKETS):
                start, length = pad_start_ref[b], pad_len_ref[b]
                head = jnp.minimum((-start) & (SUBLANES - 1), length)
                for k in range(SUBLANES - 1):
                    @pl.when(k < head)
                    def _(start=start, k=k):
                        zero_copy(start + k, 1)
                start = start + head
                length = length - head
                piece = half
                while piece >= SUBLANES:
                    @pl.when((length & piece) != 0)
                    def _(start=start, piece=piece):
                        zero_copy(pl.multiple_of(start, SUBLANES), piece)
                    start = start + (length & piece)
                    piece //= 2
            for k in range(N_BUCKETS):
                @pl.when(n_tiles - 1 - k >= n_live)
                def _(k=k):
                    zero_copy((n_tiles - 1 - k) * 2 * half, half)
                    zero_copy((n_tiles - 1 - k) * 2 * half + half, half)

        pad_copies(lambda copy: copy.start())
        pad_copies(lambda copy: copy.wait())

    pltpu.make_async_copy(hx_ref, xs_hbm.at[pl.ds(0, chunk)], sem).wait()


def _dispatch(pos, pad_start, pad_len, hx, n_rows, *, tms):
    n, w = hx.shape
    chunk = DISPATCH_CHUNK
    return pl.pallas_call(
        _dispatch_kernel,
        grid_spec=pltpu.PrefetchScalarGridSpec(
            num_scalar_prefetch=2,
            grid=(n // chunk,),
            in_specs=[pl.BlockSpec((None, 1, chunk), lambda i, ps, pn: (i, 0, 0), memory_space=pltpu.SMEM),
                      pl.BlockSpec((chunk, w), lambda i, ps, pn: (i, 0))],
            out_specs=pl.BlockSpec(memory_space=pl.ANY),
            scratch_shapes=[pltpu.VMEM((tms // 2, w), F32), pltpu.SemaphoreType.DMA(()),
                            pltpu.SemaphoreType.DMA(())],
        ),
        out_shape=jax.ShapeDtypeStruct((n_rows, w), F32),
        compiler_params=_params("arbitrary"),
        name="moe_dispatch",
    )(pad_start, pad_len, pos.reshape(n // chunk, 1, chunk), hx)


def _experts_kernel(blk_ref, ea_ref, eb_ref, nv_ref, xs_ref, wga_ref, wua_ref, wda_ref, wgb_ref, wub_ref, wdb_ref,
                    ys_ref):
    nv = nv_ref[pl.program_id(0)]

    @pl.when(nv == 0)
    def _():
        ys_ref[...] = jnp.zeros_like(ys_ref)

    @pl.when(nv > 0)
    def _():
        x = xs_ref[:, :D_MODEL].astype(BF16)
        g_lo = xs_ref[:, D_MODEL:D_MODEL + 1]
        g_hi = xs_ref[:, D_MODEL + 1:D_MODEL + 2]

        def ffn(wg_ref, wu_ref, wd_ref):
            gate = jnp.dot(x, wg_ref[...], preferred_element_type=F32)
            up = jnp.dot(x, wu_ref[...], preferred_element_type=F32)
            he = (gate * _sigmoid(gate) * up).astype(BF16)
            return jnp.dot(he, wd_ref[...], preferred_element_type=F32)

        ys_ref[...] = g_lo * ffn(wga_ref, wua_ref, wda_ref) + g_hi * ffn(wgb_ref, wub_ref, wdb_ref)


def _experts(tile_blk, tile_ea, tile_eb, tile_nv, xs, wg, wu, wd, *, tms):
    n_rows, w = xs.shape
    d = D_MODEL
    n_tiles = n_rows // tms
    wspec = lambda shape, which: pl.BlockSpec(
        (None,) + shape, lambda j, blk, ea, eb, nv: ((ea, eb)[which][j], 0, 0))
    return pl.pallas_call(
        _experts_kernel,
        grid_spec=pltpu.PrefetchScalarGridSpec(
            num_scalar_prefetch=4,
            grid=(n_tiles,),
            in_specs=[
                pl.BlockSpec((tms, w), lambda j, blk, ea, eb, nv: (blk[j], 0)),
                wspec((d, D_EXPERT), 0), wspec((d, D_EXPERT), 0), wspec((D_EXPERT, d), 0),
                wspec((d, D_EXPERT), 1), wspec((d, D_EXPERT), 1), wspec((D_EXPERT, d), 1),
            ],
            out_specs=pl.BlockSpec((tms, d), lambda j, blk, ea, eb, nv: (j, 0)),
        ),
        out_shape=jax.ShapeDtypeStruct((n_rows, d), F32),
        compiler_params=_params("arbitrary"),
        name="moe_experts",
    )(tile_blk, tile_ea, tile_eb, tile_nv, xs, wg, wu, wd, wg, wu, wd)


def _combine_kernel(pos_ref, pos_next_ref, ys_hbm, x1_ref, m_ref, lng_ref, lnb_ref, o_ref, buf_ref, sem):
    step = pl.program_id(0)
    tm = buf_ref.shape[1]

    def gather(p_ref, slot):
        def issue(g, carry):
            base = pl.multiple_of(g * SUBLANES, SUBLANES)
            for u in range(SUBLANES):
                pltpu.make_async_copy(ys_hbm.at[pl.ds(p_ref[0, base + u], 1)],
                                      buf_ref.at[slot, pl.ds(base + u, 1)], sem.at[slot]).start()
            return carry

        lax.fori_loop(0, tm // SUBLANES, issue, 0, unroll=2)

    @pl.when(step == 0)
    def _():
        gather(pos_ref, 0)

    slot = step % 2

    @pl.when(step + 1 < pl.num_programs(0))
    def _():
        gather(pos_next_ref, 1 - slot)

    pltpu.make_async_copy(ys_hbm.at[pl.ds(0, tm)], buf_ref.at[slot], sem.at[slot]).wait()
    o_ref[...] = _ln(ALPHA * x1_ref[...] + m_ref[5:6, :] * buf_ref[slot]) * lng_ref[...] + lnb_ref[...]


def _combine(pos, ys, x1, mod_l, lng, lnb, *, cond_row):
    n, d = x1.shape
    tm = TM_COMBINE
    n_tiles = n // tm
    vec = pl.BlockSpec((1, d), lambda i: (0, 0))
    pos3 = pos.reshape(n_tiles, 1, tm)
    return pl.pallas_call(
        _combine_kernel,
        grid=(n_tiles,),
        in_specs=[
            pl.BlockSpec((None, 1, tm), lambda i: (i, 0, 0), memory_space=pltpu.SMEM),
            pl.BlockSpec((None, 1, tm), lambda i: (jnp.minimum(i + 1, n_tiles - 1), 0, 0),
                         memory_space=pltpu.SMEM),
            pl.BlockSpec(memory_space=pl.ANY),
            pl.BlockSpec((tm, d), lambda i: (i, 0)),
            pl.BlockSpec((None, N_MOD_PAD, d), lambda i: (cond_row(i), 0, 0)),
            vec, vec,
        ],
        out_specs=pl.BlockSpec((tm, d), lambda i: (i, 0)),
        out_shape=jax.ShapeDtypeStruct((n, d), F32),
        scratch_shapes=[pltpu.VMEM((2, tm, d), F32), pltpu.SemaphoreType.DMA((2,))],
        compiler_params=_params("arbitrary"),
        name="moe_combine_ln2",
    )(pos3, pos3, ys, x1, mod_l, lng, lnb)


def _routing_plan(meta, counts, *, tms):
    b, _, n = meta.shape
    n_tok = b * n
    bucket = meta[:, 0, :].reshape(n_tok).astype(jnp.int32)
    rank = meta[:, 1, :].reshape(n_tok).astype(jnp.int32)
    cnt = counts[:N_BUCKETS, 0].astype(jnp.int32)
    tiles = (cnt + tms - 1) // tms
    tile_end = jnp.cumsum(tiles)
    tile_off = tile_end - tiles
    n_tiles = n_tok // tms + N_BUCKETS
    ids = jnp.arange(N_BUCKETS, dtype=jnp.int32)
    pos = rank + jnp.sum(jnp.where(bucket[:, None] == ids[None, :], (tile_off * tms)[None, :], 0), axis=1)
    j = jnp.arange(n_tiles, dtype=jnp.int32)
    live = j < tile_end[-1]
    jb = jnp.minimum(j, tile_end[-1] - 1)
    tb = jnp.sum((jb[:, None] >= tile_end[None, :]).astype(jnp.int32), axis=1)
    pick = lambda table: jnp.sum(jnp.where(tb[:, None] == ids[None, :], table[None, :], 0), axis=1)
    nv = jnp.where(live, jnp.clip(pick(cnt) - (jb - pick(tile_off)) * tms, 0, tms), 0)
    pair = ids % PAIRS_PER_GROUP
    lo = jnp.where(pair < 3, 0, jnp.where(pair < 5, 1, 2))
    hi = jnp.where(pair < 3, pair + 1, jnp.where(pair < 5, pair - 1, 3))
    group = ids // PAIRS_PER_GROUP
    ea = pick(group * EXPERTS_PER_GROUP + lo)
    eb = pick(group * EXPERTS_PER_GROUP + hi)
    pad_start = jnp.concatenate([tile_off * tms + cnt, tile_end[-1:]]).astype(jnp.int32)
    pad_len = jnp.concatenate([tiles * tms - cnt, jnp.zeros((1,), jnp.int32)]).astype(jnp.int32)
    return pos, (pad_start, pad_len), (jb, ea, eb, nv.astype(jnp.int32)), n_tiles * tms


def _moe(hx, meta, counts, x1, moe_w, mod_l, lng, lnb, *, tms, cond_row):
    d = D_MODEL
    pos, pads, tile_tables, n_rows = _routing_plan(meta, counts, tms=tms)
    xs = _dispatch(pos, *pads, hx.reshape(-1, HX_W), n_rows, tms=tms)
    ys = _experts(*tile_tables, xs, *moe_w, tms=tms)
    return _combine(pos, ys, x1.reshape(-1, d), mod_l, lng, lnb, cond_row=cond_row)


def _rope_tables():
    t = jnp.arange(SEQ, dtype=jnp.int32)
    pos = jnp.stack([(t // GRID_W).astype(F32), (t % GRID_W).astype(F32)], axis=1)
    inv = 1.0 / (ROPE_THETA ** (jnp.arange(N_FREQ, dtype=F32) / N_FREQ))
    ang = pos[:, :, None] * inv
    cos, sin = jnp.cos(ang), jnp.sin(ang)
    zero = jnp.zeros_like(sin)
    expand = lambda first, second: jnp.tile(
        jnp.stack([first, second], axis=2).reshape(SEQ, DQK), (1, LANES // DQK))
    return expand(cos, cos), expand(-sin, zero), expand(zero, sin)


def kernel(x, c, ctx, c_ctx, w_mod, b_mod, w_in, diff_lambda, attn_norm_g, conv_w, conv_b, w_out,
           ln1_g, ln1_b, ln2_g, ln2_b, router_w, router_bias, w_gate, w_up, w_down):
    assert x.shape == (BATCH, SEQ, D_MODEL) and ctx.shape == (BATCH, CTX_LEN, D_MODEL)
    d = D_MODEL
    cc = jnp.concatenate([c, c_ctx[None, :], jnp.zeros((N_COND_PAD - N_COND, d), F32)], axis=0)
    mod = _modulation(cc, w_mod, b_mod)
    rope_tabs = _rope_tables()
    w_in_b, w_out_b = w_in.astype(BF16), w_out.astype(BF16)
    wg_b, wu_b, wd_b = w_gate.astype(BF16), w_up.astype(BF16), w_down.astype(BF16)
    rwt = router_w.T
    rbias = router_bias.reshape(N_EXPERTS, 1)

    lat_row = lambda bi: bi
    ctx_row = lambda bi: CTX_ROW
    lat_tile_row = lambda i: i // (SEQ // TM_COMBINE)
    tri = lambda tm: (jnp.arange(tm)[:, None] <= jnp.arange(tm)[None, :]).astype(BF16)

    for l in range(DEPTH):
        last = l == DEPTH - 1
        lam_init = 0.8 - 0.6 * math.exp(-0.3 * l)
        vec = lambda p, w: p[l].reshape(1, w)
        post = (w_out_b[l], conv_w[l], vec(conv_b, D_CONV), vec(ln1_g, d), vec(ln1_b, d), rwt, rbias)
        ln2 = (vec(ln2_g, d), vec(ln2_b, d))

        q, kt, v, g, gb = _inproj(x, mod[l], w_in_b[l], rope_tabs, tm=TM_LAT, cond_row=lat_row)
        if last:
            ktc, vc = _inproj(ctx, mod[l], w_in_b[l][:, D_ATT:3 * D_ATT], None, tm=TM_CTX, cond_row=ctx_row,
                              kv_only=True)
        else:
            qc, ktc, vc, gc, gbc = _inproj(ctx, mod[l], w_in_b[l], None, tm=TM_CTX, cond_row=ctx_row)

        att = _attention(diff_lambda[l], q, [(kt, v), (ktc, vc)], attn_norm_g[l], lam_init=lam_init)
        x1, hx, meta, counts = _outproj(att, g, gb, x, mod[l], *post, tri(TM_LAT), tm=TM_LAT, cond_row=lat_row)
        moe_w = (wg_b[l], wu_b[l], wd_b[l])
        x = _moe(hx, meta, counts, x1, moe_w, mod[l], *ln2, tms=TMS_LAT, cond_row=lat_tile_row
                 ).reshape(BATCH, SEQ, d)
        if not last:
            att_c = _attention(diff_lambda[l], qc, [(ktc, vc)], attn_norm_g[l], lam_init=lam_init)
            c1, hxc, meta_c, counts_c = _outproj(att_c, gc, gbc, ctx, mod[l], *post, tri(TM_CTX), tm=TM_CTX,
                                                 cond_row=ctx_row)
            ctx = _moe(hxc, meta_c, counts_c, c1, moe_w, mod[l], *ln2, tms=TMS_CTX, cond_row=ctx_row
                       ).reshape(BATCH, CTX_LEN, d)
    return x
```

```python
import functools
import math

import jax
import jax.numpy as jnp
from jax import lax
from jax.experimental import pallas as pl
from jax.experimental.pallas import tpu as pltpu

D_MODEL = 1024
BATCH = 16
SEQ = 4096
DEPTH = 2
GRID_W = 64
CTX_LEN = 256
D_ATT = D_MODEL // 2
D_CONV = D_MODEL - D_ATT
N_HEADS = 4
DV = D_ATT // N_HEADS
DQK = DV // 2
N_FREQ = DQK // 4
ROPE_THETA = 10000.0
N_EXPERTS = 16
N_GROUPS = 4
EXPERTS_PER_GROUP = N_EXPERTS // N_GROUPS
D_EXPERT = 512
LN_EPS = 1e-6
HEAD_NORM_EPS = 1e-5
ALPHA = (2 * DEPTH) ** 0.25

F32 = jnp.float32
BF16 = jnp.bfloat16

LANES = 128
SUBLANES = 8
VMEM_LIMIT_BYTES = 58 * 1024 * 1024

N_COND = BATCH + 1
N_COND_PAD = 24
CTX_ROW = BATCH
N_MOD = 6
N_MOD_PAD = SUBLANES

TM_LAT = 512
TM_CTX = CTX_LEN
TQ = 256
N_SUB = 4
MIN_ROW_SUM = 2.0 ** -100
RB = 16
LOG2E = math.log2(math.e)
PAIRS_PER_GROUP = EXPERTS_PER_GROUP * (EXPERTS_PER_GROUP - 1) // 2
N_BUCKETS = N_GROUPS * PAIRS_PER_GROUP
N_BUCKET_PAD = 32
HX_W = D_MODEL + LANES
TMS_LAT = 512
TMS_CTX = 256
DISPATCH_CHUNK = 2048
TM_COMBINE = 512


def _params(*sem):
    return pltpu.CompilerParams(dimension_semantics=sem, vmem_limit_bytes=VMEM_LIMIT_BYTES)


def _ln(x):
    mu = jnp.mean(x, axis=-1, keepdims=True)
    xc = x - mu
    var = jnp.mean(xc * xc, axis=-1, keepdims=True)
    return xc * lax.rsqrt(var + LN_EPS)


def _sigmoid(x):
    return 1.0 / (1.0 + jnp.exp(-x))


def _split_bf16(x):
    hi = x.astype(BF16)
    lo = (x - hi.astype(F32)).astype(BF16)
    return hi, lo


def _mod_kernel(cc_ref, w_ref, b_ref, o_ref):
    a = cc_ref[...]
    a = a * _sigmoid(a)
    a_hi, a_lo = _split_bf16(a)
    w_hi, w_lo = _split_bf16(w_ref[...])
    m = (jnp.dot(a_hi, w_hi, preferred_element_type=F32)
         + jnp.dot(a_lo, w_hi, preferred_element_type=F32)
         + jnp.dot(a_hi, w_lo, preferred_element_type=F32))
    o_ref[...] = m + b_ref[...]


def _modulation(cc, w_mod, b_mod):
    d = D_MODEL
    out = pl.pallas_call(
        _mod_kernel,
        grid=(DEPTH, N_MOD),
        in_specs=[
            pl.BlockSpec((N_COND_PAD, d), lambda l, j: (0, 0)),
            pl.BlockSpec((None, d, d), lambda l, j: (l, 0, j)),
            pl.BlockSpec((None, 1, d), lambda l, j: (l, 0, j)),
        ],
        out_specs=pl.BlockSpec((None, N_COND_PAD, d), lambda l, j: (l, 0, j)),
        out_shape=jax.ShapeDtypeStruct((DEPTH, N_COND_PAD, N_MOD * d), F32),
        compiler_params=_params("parallel", "parallel"),
        name="modulation",
    )(cc, w_mod, b_mod.reshape(DEPTH, 1, N_MOD * d))
    m = out.reshape(DEPTH, N_COND_PAD, N_MOD, d)
    return jnp.pad(m, ((0, 0), (0, 0), (0, N_MOD_PAD - N_MOD), (0, 0)))


def _inproj_kernel(*refs, use_rope, kv_only):
    x_ref, m_ref, w_ref = refs[:3]
    rest = refs[3:]
    if use_rope:
        ra_ref, rb_ref, rc_ref = rest[:3]
        rest = rest[3:]
    u = _ln(x_ref[...]) * (1.0 + m_ref[1:2, :]) + m_ref[0:1, :]
    y = jnp.dot(u.astype(BF16), w_ref[...], preferred_element_type=F32)
    if kv_only:
        kt_ref, v_ref = rest
        kt_ref[...] = y[:, :D_ATT].T.astype(BF16)
        v_ref[...] = y[:, D_ATT:2 * D_ATT].astype(BF16)
        return
    q_ref, kt_ref, v_ref, g_ref, gb_ref = rest
    q = y[:, :D_ATT] * (DQK ** -0.5 * LOG2E)
    k = y[:, D_ATT:2 * D_ATT]
    if use_rope:
        ra, rb, rc = ra_ref[...], rb_ref[...], rc_ref[...]

        def rope(t):
            cols = []
            for j in range(D_ATT // LANES):
                tj = t[:, j * LANES:(j + 1) * LANES]
                cols.append(tj * ra
                            + pltpu.roll(tj, LANES - N_FREQ, 1) * rb
                            + pltpu.roll(tj, N_FREQ, 1) * rc)
            return jnp.concatenate(cols, axis=1)

        q = rope(q)
        k = rope(k)
    q_ref[...] = q.astype(BF16)
    kt_ref[...] = k.T.astype(BF16)
    v_ref[...] = y[:, 2 * D_ATT:3 * D_ATT].astype(BF16)
    gb_ref[...] = y[:, 3 * D_ATT:3 * D_ATT + D_CONV]
    g_ref[...] = y[:, 3 * D_ATT + D_CONV:3 * D_ATT + 2 * D_CONV] * y[:, 3 * D_ATT + 2 * D_CONV:]


def _inproj(x, mod_l, w, rope_tabs, *, tm, cond_row, kv_only=False):
    b, n, d = x.shape
    use_rope = rope_tabs is not None
    wn = w.shape[1]
    in_specs = [
        pl.BlockSpec((None, tm, d), lambda bi, i: (bi, i, 0)),
        pl.BlockSpec((None, N_MOD_PAD, d), lambda bi, i: (cond_row(bi), 0, 0)),
        pl.BlockSpec((d, wn), lambda bi, i: (0, 0)),
    ]
    args = [x, mod_l, w]
    if use_rope:
        in_specs += [pl.BlockSpec((tm, LANES), lambda bi, i: (i, 0))] * 3
        args += list(rope_tabs)
    kt_spec = pl.BlockSpec((None, D_ATT, tm), lambda bi, i: (bi, 0, i))
    row_spec = pl.BlockSpec((None, tm, D_ATT), lambda bi, i: (bi, i, 0))
    kt_shape = jax.ShapeDtypeStruct((b, D_ATT, n), BF16)
    if kv_only:
        out_specs = [kt_spec, row_spec]
        out_shape = [kt_shape, jax.ShapeDtypeStruct((b, n, D_ATT), BF16)]
    else:
        out_specs = [row_spec, kt_spec, row_spec, row_spec, row_spec]
        out_shape = [jax.ShapeDtypeStruct((b, n, D_ATT), BF16), kt_shape,
                     jax.ShapeDtypeStruct((b, n, D_ATT), BF16),
                     jax.ShapeDtypeStruct((b, n, D_CONV), F32),
                     jax.ShapeDtypeStruct((b, n, D_CONV), F32)]
    return pl.pallas_call(
        functools.partial(_inproj_kernel, use_rope=use_rope, kv_only=kv_only),
        grid=(b, n // tm),
        in_specs=in_specs,
        out_specs=out_specs,
        out_shape=out_shape,
        compiler_params=_params("parallel", "parallel"),
        name="inproj",
    )(*args)


def _attn_kernel(*refs, n_seg, lam_init):
    lp_ref, q_ref = refs[:2]
    seg_refs = refs[2:2 + 2 * n_seg]
    gn_ref, o_ref, s_ref, a_ref, ml_ref, r_ref, kn_ref = refs[2 + 2 * n_seg:]
    kt_refs = seg_refs[0::2]
    v_refs = seg_refs[1::2]
    n_sub, two_tq, _ = s_ref.shape
    tq = two_tq // 2

    lp = lp_ref[...]
    lam = (jnp.exp(jnp.sum(lp[0:1, :] * lp[1:2, :], axis=-1, keepdims=True))
           - jnp.exp(jnp.sum(lp[2:3, :] * lp[3:4, :], axis=-1, keepdims=True)) + lam_init)

    @pl.when(pl.program_id(2) == 0)
    def _():
        k1, k2 = None, None
        for kt_ref in kt_refs:
            ksq = jnp.square(kt_ref[...].astype(F32))
            m1 = jnp.max(jnp.sum(ksq[:DQK, :], axis=0, keepdims=True), axis=-1, keepdims=True)
            m2 = jnp.max(jnp.sum(ksq[DQK:, :], axis=0, keepdims=True), axis=-1, keepdims=True)
            k1 = m1 if k1 is None else jnp.maximum(k1, m1)
            k2 = m2 if k2 is None else jnp.maximum(k2, m2)
        kn_ref[0:1, :] = jnp.broadcast_to(k1, (1, LANES))
        kn_ref[1:2, :] = jnp.broadcast_to(k2, (1, LANES))

    def stacked_q(j):
        q = q_ref[j * tq:(j + 1) * tq, :]
        lane = lax.broadcasted_iota(jnp.int32, q.shape, 1)
        zero = jnp.zeros_like(q)
        return jnp.concatenate([jnp.where(lane < DQK, q, zero), jnp.where(lane >= DQK, q, zero)], axis=0)

    def scores(j):
        qz = stacked_q(j)
        off = 0
        for kt_ref in kt_refs:
            nk = kt_ref.shape[1]
            s_ref[j, :, off:off + nk] = jnp.dot(qz, kt_ref[...], preferred_element_type=F32)
            off += nk

    nk_total = s_ref.shape[2]
    n_rep = nk_total // LANES

    def softmax_bounded(j):
        qz = stacked_q(j)
        qsq = jnp.sum(jnp.square(qz.astype(F32)), axis=-1, keepdims=True)
        knorm = jnp.concatenate([jnp.broadcast_to(kn_ref[0:1, :], (tq, LANES)),
                                 jnp.broadcast_to(kn_ref[1:2, :], (tq, LANES))], axis=0)
        bound = jnp.sqrt(qsq * knorm)
        total = None
        off = 0
        for kt_ref in kt_refs:
            nk = kt_ref.shape[1]
            e = jnp.exp2(jnp.dot(qz, kt_ref[...], preferred_element_type=F32)
                         - jnp.concatenate([bound] * (nk // LANES), axis=1))
            s_ref[j, :, off:off + nk] = e
            part = jnp.sum(e, axis=-1, keepdims=True)
            total = part if total is None else total + part
            off += nk
        ml_ref[j] = jnp.broadcast_to(total, (two_tq, LANES))
        return jnp.min(total)

    def softmax_exact(j):
        for g in range(two_tq // SUBLANES):
            rows = slice(g * SUBLANES, (g + 1) * SUBLANES)
            m = jnp.max(s_ref[j, rows, :], axis=-1, keepdims=True)
            ml_ref[j, rows, :] = jnp.broadcast_to(m, (SUBLANES, LANES))
        for g in range(two_tq // SUBLANES):
            rows = slice(g * SUBLANES, (g + 1) * SUBLANES)
            e = jnp.exp2(s_ref[j, rows, :] - jnp.concatenate([ml_ref[j, rows, :]] * n_rep, axis=1))
            s_ref[j, rows, :] = e
            ml_ref[j, rows, :] = jnp.broadcast_to(jnp.sum(e, axis=-1, keepdims=True), (SUBLANES, LANES))

    def combine(j):
        for r in range(tq // RB):
            rows1 = slice(r * RB, (r + 1) * RB)
            rows2 = slice(tq + r * RB, tq + (r + 1) * RB)
            l1 = ml_ref[j, rows1, :]
            c = jnp.concatenate([lam * l1 / ml_ref[j, rows2, :]] * n_rep, axis=1)
            a_ref[j, rows1, :] = (s_ref[j, rows1, :] - c * s_ref[j, rows2, :]).astype(BF16)
            r_ref[j, rows1, :] = 1.0 / l1

    def values(j):
        o = None
        off = 0
        for v_ref in v_refs:
            nk = v_ref.shape[0]
            part = jnp.dot(a_ref[j, :, off:off + nk], v_ref[...], preferred_element_type=F32)
            o = part if o is None else o + part
            off += nk
        o = o * r_ref[j]
        y = o * lax.rsqrt(jnp.mean(o * o, axis=-1, keepdims=True) + HEAD_NORM_EPS)
        o_ref[j * tq:(j + 1) * tq, :] = (y * gn_ref[...] * (1.0 - lam_init)).astype(BF16)

    smallest = None
    for j in range(n_sub):
        low = softmax_bounded(j)
        smallest = low if smallest is None else jnp.minimum(smallest, low)

    @pl.when(jnp.logical_not(smallest >= MIN_ROW_SUM))
    def _():
        for j in range(n_sub):
            scores(j)
            softmax_exact(j)

    for j in range(n_sub):
        combine(j)
        values(j)


def _attention(lam_params, q, segs, attn_g, *, lam_init):
    b, n, _ = q.shape
    tq = min(TQ, n)
    n_sub = min(N_SUB, n // tq)
    tstep = n_sub * tq
    nk_total = sum(kt.shape[2] for kt, _ in segs)
    in_specs = [
        pl.BlockSpec((4, DQK), lambda bi, h, i: (0, 0)),
        pl.BlockSpec((None, tstep, DV), lambda bi, h, i: (bi, i, h)),
    ]
    args = [lam_params, q]
    for kt, v in segs:
        nk = kt.shape[2]
        in_specs += [pl.BlockSpec((None, DV, nk), lambda bi, h, i: (bi, h, 0)),
                     pl.BlockSpec((None, nk, DV), lambda bi, h, i: (bi, 0, h))]
        args += [kt, v]
    in_specs.append(pl.BlockSpec((1, DV), lambda bi, h, i: (0, h)))
    args.append(attn_g.reshape(1, D_ATT))
    return pl.pallas_call(
        functools.partial(_attn_kernel, n_seg=len(segs), lam_init=lam_init),
        grid=(b, N_HEADS, n // tstep),
        in_specs=in_specs,
        out_specs=pl.BlockSpec((None, tstep, DV), lambda bi, h, i: (bi, i, h)),
        out_shape=jax.ShapeDtypeStruct((b, n, D_ATT), BF16),
        scratch_shapes=[pltpu.VMEM((n_sub, 2 * tq, nk_total), F32),
                        pltpu.VMEM((n_sub, tq, nk_total), BF16),
                        pltpu.VMEM((n_sub, 2 * tq, LANES), F32),
                        pltpu.VMEM((n_sub, tq, LANES), F32),
                        pltpu.VMEM((SUBLANES, LANES), F32)],
        compiler_params=_params("parallel", "parallel", "arbitrary"),
        name="diff_attention",
    )(*args)


def _top2_sum(a, b, c, d):
    hi1, lo1 = jnp.maximum(a, b), jnp.minimum(a, b)
    hi2, lo2 = jnp.maximum(c, d), jnp.minimum(c, d)
    return jnp.maximum(hi1, hi2) + jnp.maximum(jnp.minimum(hi1, hi2), jnp.maximum(lo1, lo2))


def _router(h, rwt_ref, rbias_ref):
    tm = h.shape[0]
    h_hi, h_lo = _split_bf16(h)
    w_hi, w_lo = _split_bf16(rwt_ref[...])
    nt = (((1,), (1,)), ((), ()))
    logits = (lax.dot_general(w_hi, h_hi, nt, preferred_element_type=F32)
              + lax.dot_general(w_hi, h_lo, nt, preferred_element_type=F32)
              + lax.dot_general(w_lo, h_hi, nt, preferred_element_type=F32))
    scores = _sigmoid(logits)
    sel = scores + rbias_ref[...]
    srow = [sel[e:e + 1, :] for e in range(N_EXPERTS)]
    crow = [scores[e:e + 1, :] for e in range(N_EXPERTS)]
    gscore = [_top2_sum(*srow[EXPERTS_PER_GROUP * g:EXPERTS_PER_GROUP * (g + 1)]) for g in range(N_GROUPS)]
    best = jnp.zeros((1, tm), jnp.int32)
    best_v = gscore[0]
    for g in range(1, N_GROUPS):
        upd = gscore[g] > best_v
        best = jnp.where(upd, g, best)
        best_v = jnp.where(upd, gscore[g], best_v)
    cand, cand_score = [], []
    for j in range(EXPERTS_PER_GROUP):
        cs, cc = srow[j], crow[j]
        for g in range(1, N_GROUPS):
            cs = jnp.where(best == g, srow[EXPERTS_PER_GROUP * g + j], cs)
            cc = jnp.where(best == g, crow[EXPERTS_PER_GROUP * g + j], cc)
        cand.append(cs)
        cand_score.append(cc)
    i0 = jnp.zeros((1, tm), jnp.int32)
    v0, w0 = cand[0], cand_score[0]
    for j in range(1, EXPERTS_PER_GROUP):
        upd = cand[j] > v0
        i0 = jnp.where(upd, j, i0)
        v0 = jnp.where(upd, cand[j], v0)
        w0 = jnp.where(upd, cand_score[j], w0)
    i1 = jnp.zeros((1, tm), jnp.int32)
    v1 = jnp.full((1, tm), -jnp.inf, F32)
    w1 = jnp.zeros((1, tm), F32)
    for j in range(EXPERTS_PER_GROUP):
        upd = jnp.logical_and(i0 != j, cand[j] > v1)
        i1 = jnp.where(upd, j, i1)
        v1 = jnp.where(upd, cand[j], v1)
        w1 = jnp.where(upd, cand_score[j], w1)
    wsum = w0 + w1
    first_lo = i0 < i1
    lo = jnp.where(first_lo, i0, i1)
    hi = jnp.where(first_lo, i1, i0)
    g_lo = jnp.where(first_lo, w0, w1) / wsum
    g_hi = jnp.where(first_lo, w1, w0) / wsum
    pair = jnp.where(lo == 0, hi - 1, jnp.where(lo == 1, hi + 1, PAIRS_PER_GROUP - 1))
    return best * PAIRS_PER_GROUP + pair, g_lo, g_hi


def _outproj_kernel(att_ref, g_ref, gprev_ref, gnext_ref, gb_ref, x_ref, m_ref, wo_ref, cw_ref, cb_ref,
                    lng_ref, lnb_ref, rwt_ref, rbias_ref, tri_ref, x1_ref, hx_ref, meta_ref, cnt_ref, run_ref,
                    *, n_tiles):
    i = pl.program_id(1)

    @pl.when(jnp.logical_and(pl.program_id(0) == 0, i == 0))
    def _():
        run_ref[...] = jnp.zeros_like(run_ref)

    g = g_ref[...]
    tm = g.shape[0]
    rows = lax.broadcasted_iota(jnp.int32, g.shape, 0)
    prev_row = jnp.where(i > 0, gprev_ref[SUBLANES - 1:SUBLANES, :], 0.0)
    next_row = jnp.where(i < n_tiles - 1, gnext_ref[0:1, :], 0.0)
    g_m1 = jnp.where(rows == 0, prev_row, pltpu.roll(g, 1, 0))
    g_p1 = jnp.where(rows == tm - 1, next_row, pltpu.roll(g, tm - 1, 0))
    conv = cw_ref[0:1, :] * g_m1 + cw_ref[1:2, :] * g + cw_ref[2:3, :] * g_p1 + cb_ref[...]
    yc = (gb_ref[...] * conv).astype(BF16)
    y = (jnp.dot(att_ref[...], wo_ref[:D_ATT, :], preferred_element_type=F32)
         + jnp.dot(yc, wo_ref[D_ATT:, :], preferred_element_type=F32))
    x1 = _ln(ALPHA * x_ref[...] + m_ref[2:3, :] * y) * lng_ref[...] + lnb_ref[...]
    x1_ref[...] = x1
    h = _ln(x1) * (1.0 + m_ref[4:5, :]) + m_ref[3:4, :]
    bucket, g_lo, g_hi = _router(h, rwt_ref, rbias_ref)
    onehot = lax.broadcasted_iota(jnp.int32, (N_BUCKET_PAD, tm), 0) == bucket
    onehot_f = jnp.where(onehot, 1.0, 0.0)
    csum = jnp.dot(onehot_f.astype(BF16), tri_ref[...], preferred_element_type=F32)
    run = run_ref[...]
    rank = jnp.sum(jnp.where(onehot, csum - 1.0 + run[:, 0:1], 0.0), axis=0, keepdims=True)
    run = run + jnp.sum(onehot_f, axis=1, keepdims=True)
    run_ref[...] = run
    cnt_ref[...] = run
    rows8 = lax.broadcasted_iota(jnp.int32, (SUBLANES, tm), 0)
    meta = jnp.where(rows8 == 0, bucket.astype(F32), jnp.where(rows8 == 1, rank, 0.0))
    meta_ref[...] = meta
    gate_rows = jnp.where(rows8 == 0, g_lo, jnp.where(rows8 == 1, g_hi, 0.0))
    gate_cols = jnp.concatenate([gate_rows, jnp.zeros((LANES - SUBLANES, tm), F32)], axis=0).T
    hx_ref[:, :D_MODEL] = h
    hx_ref[:, D_MODEL:] = gate_cols


def _outproj(att, g, gb, x, mod_l, wo, cw, cb, lng, lnb, rwt, rbias, tri, *, tm, cond_row):
    b, n, d = x.shape
    n_tiles = n // tm
    g8 = g.reshape(b, n // SUBLANES, SUBLANES, D_CONV)
    tpb = tm // SUBLANES
    row = lambda w: pl.BlockSpec((None, tm, w), lambda bi, i: (bi, i, 0))
    full = lambda s: pl.BlockSpec(s, lambda bi, i: (0,) * len(s))
    return pl.pallas_call(
        functools.partial(_outproj_kernel, n_tiles=n_tiles),
        grid=(b, n_tiles),
        in_specs=[
            row(D_ATT), row(D_CONV),
            pl.BlockSpec((None, None, SUBLANES, D_CONV), lambda bi, i: (bi, jnp.maximum(i * tpb - 1, 0), 0, 0)),
            pl.BlockSpec((None, None, SUBLANES, D_CONV),
                         lambda bi, i: (bi, jnp.minimum((i + 1) * tpb, n // SUBLANES - 1), 0, 0)),
            row(D_CONV), row(d),
            pl.BlockSpec((None, N_MOD_PAD, d), lambda bi, i: (cond_row(bi), 0, 0)),
            full((d, d)), full((3, D_CONV)), full((1, D_CONV)), full((1, d)), full((1, d)),
            full((N_EXPERTS, d)), full((N_EXPERTS, 1)), full((tm, tm)),
        ],
        out_specs=[row(d), row(HX_W), pl.BlockSpec((None, SUBLANES, tm), lambda bi, i: (bi, 0, i)),
                   full((N_BUCKET_PAD, LANES))],
        out_shape=[jax.ShapeDtypeStruct((b, n, d), F32), jax.ShapeDtypeStruct((b, n, HX_W), F32),
                   jax.ShapeDtypeStruct((b, SUBLANES, n), F32),
                   jax.ShapeDtypeStruct((N_BUCKET_PAD, LANES), F32)],
        scratch_shapes=[pltpu.VMEM((N_BUCKET_PAD, LANES), F32)],
        compiler_params=_params("arbitrary", "arbitrary"),
        name="outproj_ln1_router",
    )(att, g, g8, g8, gb, x, mod_l, wo, cw, cb, lng, lnb, rwt, rbias, tri)


def _dispatch_kernel(pad_start_ref, pad_len_ref, pos_ref, hx_ref, xs_hbm, zero_ref, sem, zero_sem):
    step = pl.program_id(0)
    chunk = hx_ref.shape[0]

    def issue(g, carry):
        base = pl.multiple_of(g * SUBLANES, SUBLANES)
        for u in range(SUBLANES):
            pltpu.make_async_copy(hx_ref.at[pl.ds(base + u, 1)], xs_hbm.at[pl.ds(pos_ref[0, base + u], 1)],
                                  sem).start()
        return carry

    lax.fori_loop(0, chunk // SUBLANES, issue, 0, unroll=2)

    @pl.when(step == 0)
    def _():
        zero_ref[...] = jnp.zeros_like(zero_ref)

        half = zero_ref.shape[0]
        n_tiles = xs_hbm.shape[0] // (2 * half)
        n_live = pad_start_ref[N_BUCKETS]

        def pad_copies(act):
            zero_copy = lambda start, rows: act(pltpu.make_async_copy(
                zero_ref.at[pl.ds(0, rows)], xs_hbm.at[pl.ds(start, rows)], zero_sem))
            for b in range(N_BUCKETS):
                start, length = pad_start_ref[b], pad_len_ref[b]
                head = jnp.minimum((-start) & (SUBLANES - 1), length)
                for k in range(SUBLANES - 1):
                    @pl.when(k < head)
                    def _(start=start, k=k):
                        zero_copy(start + k, 1)
                start = start + head
                length = length - head
                piece = half
                while piece >= SUBLANES:
                    @pl.when((length & piece) != 0)
                    def _(start=start, piece=piece):
                        zero_copy(pl.multiple_of(start, SUBLANES), piece)
                    start = start + (length & piece)
                    piece //= 2
            for k in range(N_BUCKETS):
                @pl.when(n_tiles - 1 - k >= n_live)
                def _(k=k):
                    zero_copy((n_tiles - 1 - k) * 2 * half, half)
                    zero_copy((n_tiles - 1 - k) * 2 * half + half, half)

        pad_copies(lambda copy: copy.start())
        pad_copies(lambda copy: copy.wait())

    pltpu.make_async_copy(hx_ref, xs_hbm.at[pl.ds(0, chunk)], sem).wait()


def _dispatch(pos, pad_start, pad_len, hx, n_rows, *, tms):
    n, w = hx.shape
    chunk = DISPATCH_CHUNK
    return pl.pallas_call(
        _dispatch_kernel,
        grid_spec=pltpu.PrefetchScalarGridSpec(
            num_scalar_prefetch=2,
            grid=(n // chunk,),
            in_specs=[pl.BlockSpec((None, 1, chunk), lambda i, ps, pn: (i, 0, 0), memory_space=pltpu.SMEM),
                      pl.BlockSpec((chunk, w), lambda i, ps, pn: (i, 0))],
            out_specs=pl.BlockSpec(memory_space=pl.ANY),
            scratch_shapes=[pltpu.VMEM((tms // 2, w), F32), pltpu.SemaphoreType.DMA(()),
                            pltpu.SemaphoreType.DMA(())],
        ),
        out_shape=jax.ShapeDtypeStruct((n_rows, w), F32),
        compiler_params=_params("arbitrary"),
        name="moe_dispatch",
    )(pad_start, pad_len, pos.reshape(n // chunk, 1, chunk), hx)


def _experts_kernel(blk_ref, ea_ref, eb_ref, nv_ref, xs_ref, wga_ref, wua_ref, wda_ref, wgb_ref, wub_ref, wdb_ref,
                    ys_ref):
    nv = nv_ref[pl.program_id(0)]

    @pl.when(nv == 0)
    def _():
        ys_ref[...] = jnp.zeros_like(ys_ref)

    @pl.when(nv > 0)
    def _():
        x = xs_ref[:, :D_MODEL].astype(BF16)
        g_lo = xs_ref[:, D_MODEL:D_MODEL + 1]
        g_hi = xs_ref[:, D_MODEL + 1:D_MODEL + 2]

        def ffn(wg_ref, wu_ref, wd_ref):
            gate = jnp.dot(x, wg_ref[...], preferred_element_type=F32)
            up = jnp.dot(x, wu_ref[...], preferred_element_type=F32)
            he = (gate * _sigmoid(gate) * up).astype(BF16)
            return jnp.dot(he, wd_ref[...], preferred_element_type=F32)

        ys_ref[...] = g_lo * ffn(wga_ref, wua_ref, wda_ref) + g_hi * ffn(wgb_ref, wub_ref, wdb_ref)


def _experts(tile_blk, tile_ea, tile_eb, tile_nv, xs, wg, wu, wd, *, layer, tms):
    n_rows, w = xs.shape
    d = D_MODEL
    n_tiles = n_rows // tms
    wspec = lambda shape, which: pl.BlockSpec(
        (None, None) + shape, lambda j, blk, ea, eb, nv: (layer, (ea, eb)[which][j], 0, 0))
    return pl.pallas_call(
        _experts_kernel,
        grid_spec=pltpu.PrefetchScalarGridSpec(
            num_scalar_prefetch=4,
            grid=(n_tiles,),
            in_specs=[
                pl.BlockSpec((tms, w), lambda j, blk, ea, eb, nv: (blk[j], 0)),
                wspec((d, D_EXPERT), 0), wspec((d, D_EXPERT), 0), wspec((D_EXPERT, d), 0),
                wspec((d, D_EXPERT), 1), wspec((d, D_EXPERT), 1), wspec((D_EXPERT, d), 1),
            ],
            out_specs=pl.BlockSpec((tms, d), lambda j, blk, ea, eb, nv: (j, 0)),
        ),
        out_shape=jax.ShapeDtypeStruct((n_rows, d), F32),
        compiler_params=_params("arbitrary"),
        name="moe_experts",
    )(tile_blk, tile_ea, tile_eb, tile_nv, xs, wg, wu, wd, wg, wu, wd)


def _combine_kernel(pos_ref, pos_next_ref, ys_hbm, x1_ref, m_ref, lng_ref, lnb_ref, o_ref, buf_ref, sem):
    step = pl.program_id(0)
    tm = buf_ref.shape[1]

    def gather(p_ref, slot):
        def issue(g, carry):
            base = pl.multiple_of(g * SUBLANES, SUBLANES)
            for u in range(SUBLANES):
                pltpu.make_async_copy(ys_hbm.at[pl.ds(p_ref[0, base + u], 1)],
                                      buf_ref.at[slot, pl.ds(base + u, 1)], sem.at[slot]).start()
            return carry

        lax.fori_loop(0, tm // SUBLANES, issue, 0, unroll=2)

    @pl.when(step == 0)
    def _():
        gather(pos_ref, 0)

    slot = step % 2

    @pl.when(step + 1 < pl.num_programs(0))
    def _():
        gather(pos_next_ref, 1 - slot)

    pltpu.make_async_copy(ys_hbm.at[pl.ds(0, tm)], buf_ref.at[slot], sem.at[slot]).wait()
    o_ref[...] = _ln(ALPHA * x1_ref[...] + m_ref[5:6, :] * buf_ref[slot]) * lng_ref[...] + lnb_ref[...]


def _combine(pos, ys, x1, mod_l, lng, lnb, *, cond_row):
    n, d = x1.shape
    tm = TM_COMBINE
    n_tiles = n // tm
    vec = pl.BlockSpec((1, d), lambda i: (0, 0))
    pos3 = pos.reshape(n_tiles, 1, tm)
    return pl.pallas_call(
        _combine_kernel,
        grid=(n_tiles,),
        in_specs=[
            pl.BlockSpec((None, 1, tm), lambda i: (i, 0, 0), memory_space=pltpu.SMEM),
            pl.BlockSpec((None, 1, tm), lambda i: (jnp.minimum(i + 1, n_tiles - 1), 0, 0),
                         memory_space=pltpu.SMEM),
            pl.BlockSpec(memory_space=pl.ANY),
            pl.BlockSpec((tm, d), lambda i: (i, 0)),
            pl.BlockSpec((None, N_MOD_PAD, d), lambda i: (cond_row(i), 0, 0)),
            vec, vec,
        ],
        out_specs=pl.BlockSpec((tm, d), lambda i: (i, 0)),
        out_shape=jax.ShapeDtypeStruct((n, d), F32),
        scratch_shapes=[pltpu.VMEM((2, tm, d), F32), pltpu.SemaphoreType.DMA((2,))],
        compiler_params=_params("arbitrary"),
        name="moe_combine_ln2",
    )(pos3, pos3, ys, x1, mod_l, lng, lnb)


def _routing_plan(meta, counts, *, tms):
    b, _, n = meta.shape
    n_tok = b * n
    bucket = meta[:, 0, :].reshape(n_tok).astype(jnp.int32)
    rank = meta[:, 1, :].reshape(n_tok).astype(jnp.int32)
    cnt = counts[:N_BUCKETS, 0].astype(jnp.int32)
    tiles = (cnt + tms - 1) // tms
    tile_end = jnp.cumsum(tiles)
    tile_off = tile_end - tiles
    n_tiles = n_tok // tms + N_BUCKETS
    ids = jnp.arange(N_BUCKETS, dtype=jnp.int32)
    pos = rank + jnp.sum(jnp.where(bucket[:, None] == ids[None, :], (tile_off * tms)[None, :], 0), axis=1)
    j = jnp.arange(n_tiles, dtype=jnp.int32)
    live = j < tile_end[-1]
    jb = jnp.minimum(j, tile_end[-1] - 1)
    tb = jnp.sum((jb[:, None] >= tile_end[None, :]).astype(jnp.int32), axis=1)
    pick = lambda table: jnp.sum(jnp.where(tb[:, None] == ids[None, :], table[None, :], 0), axis=1)
    nv = jnp.where(live, jnp.clip(pick(cnt) - (jb - pick(tile_off)) * tms, 0, tms), 0)
    pair = ids % PAIRS_PER_GROUP
    lo = jnp.where(pair < 3, 0, jnp.where(pair < 5, 1, 2))
    hi = jnp.where(pair < 3, pair + 1, jnp.where(pair < 5, pair - 1, 3))
    group = ids // PAIRS_PER_GROUP
    ea = pick(group * EXPERTS_PER_GROUP + lo)
    eb = pick(group * EXPERTS_PER_GROUP + hi)
    pad_start = jnp.concatenate([tile_off * tms + cnt, tile_end[-1:]]).astype(jnp.int32)
    pad_len = jnp.concatenate([tiles * tms - cnt, jnp.zeros((1,), jnp.int32)]).astype(jnp.int32)
    return pos, (pad_start, pad_len), (jb, ea, eb, nv.astype(jnp.int32)), n_tiles * tms


def _moe(hx, meta, counts, x1, moe_w, mod_l, lng, lnb, *, layer, tms, cond_row):
    d = D_MODEL
    pos, pads, tile_tables, n_rows = _routing_plan(meta, counts, tms=tms)
    xs = _dispatch(pos, *pads, hx.reshape(-1, HX_W), n_rows, tms=tms)
    ys = _experts(*tile_tables, xs, *moe_w, layer=layer, tms=tms)
    return _combine(pos, ys, x1.reshape(-1, d), mod_l, lng, lnb, cond_row=cond_row)


def _rope_tables():
    t = jnp.arange(SEQ, dtype=jnp.int32)
    pos = jnp.stack([(t // GRID_W).astype(F32), (t % GRID_W).astype(F32)], axis=1)
    inv = 1.0 / (ROPE_THETA ** (jnp.arange(N_FREQ, dtype=F32) / N_FREQ))
    ang = pos[:, :, None] * inv
    cos, sin = jnp.cos(ang), jnp.sin(ang)
    zero = jnp.zeros_like(sin)
    expand = lambda first, second: jnp.tile(
        jnp.stack([first, second], axis=2).reshape(SEQ, DQK), (1, LANES // DQK))
    return expand(cos, cos), expand(-sin, zero), expand(zero, sin)


def kernel(x, c, ctx, c_ctx, w_mod, b_mod, w_in, diff_lambda, attn_norm_g, conv_w, conv_b, w_out,
           ln1_g, ln1_b, ln2_g, ln2_b, router_w, router_bias, w_gate, w_up, w_down):
    assert x.shape == (BATCH, SEQ, D_MODEL) and ctx.shape == (BATCH, CTX_LEN, D_MODEL)
    d = D_MODEL
    cc = jnp.concatenate([c, c_ctx[None, :], jnp.zeros((N_COND_PAD - N_COND, d), F32)], axis=0)
    mod = _modulation(cc, w_mod, b_mod)
    rope_tabs = _rope_tables()
    w_in_b, w_out_b = w_in.astype(BF16), w_out.astype(BF16)
    wg_b, wu_b, wd_b = w_gate.astype(BF16), w_up.astype(BF16), w_down.astype(BF16)
    rwt = router_w.T
    rbias = router_bias.reshape(N_EXPERTS, 1)

    lat_row = lambda bi: bi
    ctx_row = lambda bi: CTX_ROW
    lat_tile_row = lambda i: i // (SEQ // TM_COMBINE)
    tri = lambda tm: (jnp.arange(tm)[:, None] <= jnp.arange(tm)[None, :]).astype(BF16)

    for l in range(DEPTH):
        last = l == DEPTH - 1
        lam_init = 0.8 - 0.6 * math.exp(-0.3 * l)
        vec = lambda p, w: p[l].reshape(1, w)
        post = (w_out_b[l], conv_w[l], vec(conv_b, D_CONV), vec(ln1_g, d), vec(ln1_b, d), rwt, rbias)
        ln2 = (vec(ln2_g, d), vec(ln2_b, d))

        q, kt, v, g, gb = _inproj(x, mod[l], w_in_b[l], rope_tabs, tm=TM_LAT, cond_row=lat_row)
        if last:
            ktc, vc = _inproj(ctx, mod[l], w_in_b[l][:, D_ATT:3 * D_ATT], None, tm=TM_CTX, cond_row=ctx_row,
                              kv_only=True)
        else:
            qc, ktc, vc, gc, gbc = _inproj(ctx, mod[l], w_in_b[l], None, tm=TM_CTX, cond_row=ctx_row)

        att = _attention(diff_lambda[l], q, [(kt, v), (ktc, vc)], attn_norm_g[l], lam_init=lam_init)
        x1, hx, meta, counts = _outproj(att, g, gb, x, mod[l], *post, tri(TM_LAT), tm=TM_LAT, cond_row=lat_row)
        moe_w = (wg_b, wu_b, wd_b)
        x = _moe(hx, meta, counts, x1, moe_w, mod[l], *ln2, layer=l, tms=TMS_LAT, cond_row=lat_tile_row
                 ).reshape(BATCH, SEQ, d)
        if not last:
            att_c = _attention(diff_lambda[l], qc, [(ktc, vc)], attn_norm_g[l], lam_init=lam_init)
            c1, hxc, meta_c, counts_c = _outproj(att_c, gc, gbc, ctx, mod[l], *post, tri(TM_CTX), tm=TM_CTX,
                                                 cond_row=ctx_row)
            ctx = _moe(hxc, meta_c, counts_c, c1, moe_w, mod[l], *ln2, layer=l, tms=TMS_CTX, cond_row=ctx_row
                       ).reshape(BATCH, CTX_LEN, d)
    return x
```

```python
import functools
import math

import jax
import jax.numpy as jnp
from jax import lax
from jax.experimental import pallas as pl
from jax.experimental.pallas import tpu as pltpu

D_MODEL = 1024
BATCH = 16
SEQ = 4096
DEPTH = 2
GRID_W = 64
CTX_LEN = 256
D_ATT = D_MODEL // 2
D_CONV = D_MODEL - D_ATT
N_HEADS = 4
DV = D_ATT // N_HEADS
DQK = DV // 2
N_FREQ = DQK // 4
ROPE_THETA = 10000.0
N_EXPERTS = 16
N_GROUPS = 4
EXPERTS_PER_GROUP = N_EXPERTS // N_GROUPS
D_EXPERT = 512
LN_EPS = 1e-6
HEAD_NORM_EPS = 1e-5
ALPHA = (2 * DEPTH) ** 0.25

F32 = jnp.float32
BF16 = jnp.bfloat16

LANES = 128
SUBLANES = 8
VMEM_LIMIT_BYTES = 58 * 1024 * 1024

N_COND = BATCH + 1
N_COND_PAD = 24
CTX_ROW = BATCH
N_MOD = 6
N_MOD_PAD = SUBLANES

TM_LAT = 512
TM_CTX = CTX_LEN
TQ = 256
N_SUB = 4
MIN_ROW_SUM = 2.0 ** -100
RB = 16
LOG2E = math.log2(math.e)
PAIRS_PER_GROUP = EXPERTS_PER_GROUP * (EXPERTS_PER_GROUP - 1) // 2
N_BUCKETS = N_GROUPS * PAIRS_PER_GROUP
N_BUCKET_PAD = 32
HX_W = D_MODEL + LANES
TMS_LAT = 512
TMS_CTX = 256
DISPATCH_CHUNK = 2048
TM_COMBINE = 512


def _params(*sem):
    return pltpu.CompilerParams(dimension_semantics=sem, vmem_limit_bytes=VMEM_LIMIT_BYTES)


def _ln(x):
    mu = jnp.mean(x, axis=-1, keepdims=True)
    xc = x - mu
    var = jnp.mean(xc * xc, axis=-1, keepdims=True)
    return xc * lax.rsqrt(var + LN_EPS)


def _sigmoid(x):
    return 1.0 / (1.0 + jnp.exp(-x))


def _split_bf16(x):
    hi = x.astype(BF16)
    lo = (x - hi.astype(F32)).astype(BF16)
    return hi, lo


def _mod_kernel(cc_ref, w_ref, b_ref, o_ref):
    a = cc_ref[...]
    a = a * _sigmoid(a)
    a_hi, a_lo = _split_bf16(a)
    w_hi, w_lo = _split_bf16(w_ref[...])
    m = (jnp.dot(a_hi, w_hi, preferred_element_type=F32)
         + jnp.dot(a_lo, w_hi, preferred_element_type=F32)
         + jnp.dot(a_hi, w_lo, preferred_element_type=F32))
    o_ref[...] = m + b_ref[...]


def _modulation(cc, w_mod, b_mod):
    d = D_MODEL
    out = pl.pallas_call(
        _mod_kernel,
        grid=(DEPTH, N_MOD),
        in_specs=[
            pl.BlockSpec((N_COND_PAD, d), lambda l, j: (0, 0)),
            pl.BlockSpec((None, d, d), lambda l, j: (l, 0, j)),
            pl.BlockSpec((None, 1, d), lambda l, j: (l, 0, j)),
        ],
        out_specs=pl.BlockSpec((None, N_COND_PAD, d), lambda l, j: (l, 0, j)),
        out_shape=jax.ShapeDtypeStruct((DEPTH, N_COND_PAD, N_MOD * d), F32),
        compiler_params=_params("parallel", "parallel"),
        name="modulation",
    )(cc, w_mod, b_mod.reshape(DEPTH, 1, N_MOD * d))
    m = out.reshape(DEPTH, N_COND_PAD, N_MOD, d)
    return jnp.pad(m, ((0, 0), (0, 0), (0, N_MOD_PAD - N_MOD), (0, 0)))


def _inproj_kernel(*refs, use_rope, kv_only):
    x_ref, m_ref, w_ref = refs[:3]
    rest = refs[3:]
    if use_rope:
        ra_ref, rb_ref, rc_ref = rest[:3]
        rest = rest[3:]
    u = _ln(x_ref[...]) * (1.0 + m_ref[1:2, :]) + m_ref[0:1, :]
    y = jnp.dot(u.astype(BF16), w_ref[...], preferred_element_type=F32)
    if kv_only:
        kt_ref, v_ref = rest
        kt_ref[...] = y[:, :D_ATT].T.astype(BF16)
        v_ref[...] = y[:, D_ATT:2 * D_ATT].astype(BF16)
        return
    q_ref, kt_ref, v_ref, g_ref, gb_ref = rest
    q = y[:, :D_ATT] * (DQK ** -0.5 * LOG2E)
    k = y[:, D_ATT:2 * D_ATT]
    if use_rope:
        ra, rb, rc = ra_ref[...], rb_ref[...], rc_ref[...]

        def rope(t):
            cols = []
            for j in range(D_ATT // LANES):
                tj = t[:, j * LANES:(j + 1) * LANES]
                cols.append(tj * ra
                            + pltpu.roll(tj, LANES - N_FREQ, 1) * rb
                            + pltpu.roll(tj, N_FREQ, 1) * rc)
            return jnp.concatenate(cols, axis=1)

        q = rope(q)
        k = rope(k)
    q_ref[...] = q.astype(BF16)
    kt_ref[...] = k.T.astype(BF16)
    v_ref[...] = y[:, 2 * D_ATT:3 * D_ATT].astype(BF16)
    gb_ref[...] = y[:, 3 * D_ATT:3 * D_ATT + D_CONV]
    g_ref[...] = y[:, 3 * D_ATT + D_CONV:3 * D_ATT + 2 * D_CONV] * y[:, 3 * D_ATT + 2 * D_CONV:]


def _inproj(x, mod_l, w, rope_tabs, *, tm, cond_row, kv_only=False):
    b, n, d = x.shape
    use_rope = rope_tabs is not None
    wn = w.shape[1]
    in_specs = [
        pl.BlockSpec((None, tm, d), lambda bi, i: (bi, i, 0)),
        pl.BlockSpec((None, N_MOD_PAD, d), lambda bi, i: (cond_row(bi), 0, 0)),
        pl.BlockSpec((d, wn), lambda bi, i: (0, 0)),
    ]
    args = [x, mod_l, w]
    if use_rope:
        in_specs += [pl.BlockSpec((tm, LANES), lambda bi, i: (i, 0))] * 3
        args += list(rope_tabs)
    kt_spec = pl.BlockSpec((None, D_ATT, tm), lambda bi, i: (bi, 0, i))
    row_spec = pl.BlockSpec((None, tm, D_ATT), lambda bi, i: (bi, i, 0))
    kt_shape = jax.ShapeDtypeStruct((b, D_ATT, n), BF16)
    if kv_only:
        out_specs = [kt_spec, row_spec]
        out_shape = [kt_shape, jax.ShapeDtypeStruct((b, n, D_ATT), BF16)]
    else:
        out_specs = [row_spec, kt_spec, row_spec, row_spec, row_spec]
        out_shape = [jax.ShapeDtypeStruct((b, n, D_ATT), BF16), kt_shape,
                     jax.ShapeDtypeStruct((b, n, D_ATT), BF16),
                     jax.ShapeDtypeStruct((b, n, D_CONV), F32),
                     jax.ShapeDtypeStruct((b, n, D_CONV), F32)]
    return pl.pallas_call(
        functools.partial(_inproj_kernel, use_rope=use_rope, kv_only=kv_only),
        grid=(b, n // tm),
        in_specs=in_specs,
        out_specs=out_specs,
        out_shape=out_shape,
        compiler_params=_params("parallel", "parallel"),
        name="inproj",
    )(*args)


def _attn_kernel(*refs, n_seg, lam_init):
    lp_ref, q_ref = refs[:2]
    seg_refs = refs[2:2 + 2 * n_seg]
    gn_ref, o_ref, s_ref, a_ref, ml_ref, r_ref, kn_ref = refs[2 + 2 * n_seg:]
    kt_refs = seg_refs[0::2]
    v_refs = seg_refs[1::2]
    n_sub, two_tq, _ = s_ref.shape
    tq = two_tq // 2

    lp = lp_ref[...]
    lam = (jnp.exp(jnp.sum(lp[0:1, :] * lp[1:2, :], axis=-1, keepdims=True))
           - jnp.exp(jnp.sum(lp[2:3, :] * lp[3:4, :], axis=-1, keepdims=True)) + lam_init)

    @pl.when(pl.program_id(2) == 0)
    def _():
        k1, k2 = None, None
        for kt_ref in kt_refs:
            ksq = jnp.square(kt_ref[...].astype(F32))
            m1 = jnp.max(jnp.sum(ksq[:DQK, :], axis=0, keepdims=True), axis=-1, keepdims=True)
            m2 = jnp.max(jnp.sum(ksq[DQK:, :], axis=0, keepdims=True), axis=-1, keepdims=True)
            k1 = m1 if k1 is None else jnp.maximum(k1, m1)
            k2 = m2 if k2 is None else jnp.maximum(k2, m2)
        kn_ref[0:1, :] = jnp.broadcast_to(k1, (1, LANES))
        kn_ref[1:2, :] = jnp.broadcast_to(k2, (1, LANES))

    def stacked_q(j):
        q = q_ref[j * tq:(j + 1) * tq, :]
        lane = lax.broadcasted_iota(jnp.int32, q.shape, 1)
        zero = jnp.zeros_like(q)
        return jnp.concatenate([jnp.where(lane < DQK, q, zero), jnp.where(lane >= DQK, q, zero)], axis=0)

    def scores(j):
        qz = stacked_q(j)
        off = 0
        for kt_ref in kt_refs:
            nk = kt_ref.shape[1]
            s_ref[j, :, off:off + nk] = jnp.dot(qz, kt_ref[...], preferred_element_type=F32)
            off += nk

    nk_total = s_ref.shape[2]
    n_rep = nk_total // LANES

    def softmax_bounded(j):
        qz = stacked_q(j)
        qsq = jnp.sum(jnp.square(qz.astype(F32)), axis=-1, keepdims=True)
        knorm = jnp.concatenate([jnp.broadcast_to(kn_ref[0:1, :], (tq, LANES)),
                                 jnp.broadcast_to(kn_ref[1:2, :], (tq, LANES))], axis=0)
        bound = jnp.sqrt(qsq * knorm)
        total = None
        off = 0
        for kt_ref in kt_refs:
            nk = kt_ref.shape[1]
            e = jnp.exp2(jnp.dot(qz, kt_ref[...], preferred_element_type=F32)
                         - jnp.concatenate([bound] * (nk // LANES), axis=1))
            s_ref[j, :, off:off + nk] = e
            part = jnp.sum(e, axis=-1, keepdims=True)
            total = part if total is None else total + part
            off += nk
        ml_ref[j] = jnp.broadcast_to(total, (two_tq, LANES))
        return jnp.min(total)

    def softmax_exact(j):
        for g in range(two_tq // SUBLANES):
            rows = slice(g * SUBLANES, (g + 1) * SUBLANES)
            m = jnp.max(s_ref[j, rows, :], axis=-1, keepdims=True)
            ml_ref[j, rows, :] = jnp.broadcast_to(m, (SUBLANES, LANES))
        for g in range(two_tq // SUBLANES):
            rows = slice(g * SUBLANES, (g + 1) * SUBLANES)
            e = jnp.exp2(s_ref[j, rows, :] - jnp.concatenate([ml_ref[j, rows, :]] * n_rep, axis=1))
            s_ref[j, rows, :] = e
            ml_ref[j, rows, :] = jnp.broadcast_to(jnp.sum(e, axis=-1, keepdims=True), (SUBLANES, LANES))

    def combine(j):
        for r in range(tq // RB):
            rows1 = slice(r * RB, (r + 1) * RB)
            rows2 = slice(tq + r * RB, tq + (r + 1) * RB)
            l1 = ml_ref[j, rows1, :]
            c = jnp.concatenate([lam * l1 / ml_ref[j, rows2, :]] * n_rep, axis=1)
            a_ref[j, rows1, :] = (s_ref[j, rows1, :] - c * s_ref[j, rows2, :]).astype(BF16)
            r_ref[j, rows1, :] = 1.0 / l1

    def values(j):
        o = None
        off = 0
        for v_ref in v_refs:
            nk = v_ref.shape[0]
            part = jnp.dot(a_ref[j, :, off:off + nk], v_ref[...], preferred_element_type=F32)
            o = part if o is None else o + part
            off += nk
        o = o * r_ref[j]
        y = o * lax.rsqrt(jnp.mean(o * o, axis=-1, keepdims=True) + HEAD_NORM_EPS)
        o_ref[j * tq:(j + 1) * tq, :] = (y * gn_ref[...] * (1.0 - lam_init)).astype(BF16)

    smallest = None
    for j in range(n_sub):
        low = softmax_bounded(j)
        smallest = low if smallest is None else jnp.minimum(smallest, low)

    @pl.when(jnp.logical_not(smallest >= MIN_ROW_SUM))
    def _():
        for j in range(n_sub):
            scores(j)
            softmax_exact(j)

    for j in range(n_sub):
        combine(j)
        values(j)


def _attention(lam_params, q, segs, attn_g, *, lam_init):
    b, n, _ = q.shape
    tq = min(TQ, n)
    n_sub = min(N_SUB, n // tq)
    tstep = n_sub * tq
    nk_total = sum(kt.shape[2] for kt, _ in segs)
    in_specs = [
        pl.BlockSpec((4, DQK), lambda bi, h, i: (0, 0)),
        pl.BlockSpec((None, tstep, DV), lambda bi, h, i: (bi, i, h)),
    ]
    args = [lam_params, q]
    for kt, v in segs:
        nk = kt.shape[2]
        in_specs += [pl.BlockSpec((None, DV, nk), lambda bi, h, i: (bi, h, 0)),
                     pl.BlockSpec((None, nk, DV), lambda bi, h, i: (bi, 0, h))]
        args += [kt, v]
    in_specs.append(pl.BlockSpec((1, DV), lambda bi, h, i: (0, h)))
    args.append(attn_g.reshape(1, D_ATT))
    return pl.pallas_call(
        functools.partial(_attn_kernel, n_seg=len(segs), lam_init=lam_init),
        grid=(b, N_HEADS, n // tstep),
        in_specs=in_specs,
        out_specs=pl.BlockSpec((None, tstep, DV), lambda bi, h, i: (bi, i, h)),
        out_shape=jax.ShapeDtypeStruct((b, n, D_ATT), BF16),
        scratch_shapes=[pltpu.VMEM((n_sub, 2 * tq, nk_total), F32),
                        pltpu.VMEM((n_sub, tq, nk_total), BF16),
                        pltpu.VMEM((n_sub, 2 * tq, LANES), F32),
                        pltpu.VMEM((n_sub, tq, LANES), F32),
                        pltpu.VMEM((SUBLANES, LANES), F32)],
        compiler_params=_params("parallel", "parallel", "arbitrary"),
        name="diff_attention",
    )(*args)


def _top2_sum(a, b, c, d):
    hi1, lo1 = jnp.maximum(a, b), jnp.minimum(a, b)
    hi2, lo2 = jnp.maximum(c, d), jnp.minimum(c, d)
    return jnp.maximum(hi1, hi2) + jnp.maximum(jnp.minimum(hi1, hi2), jnp.maximum(lo1, lo2))


def _router(h, rwt_ref, rbias_ref):
    tm = h.shape[0]
    nt = (((1,), (1,)), ((), ()))
    logits = lax.dot_general(rwt_ref[...].astype(BF16), h.astype(BF16), nt, preferred_element_type=F32)
    scores = _sigmoid(logits)
    sel = scores + rbias_ref[...]
    srow = [sel[e:e + 1, :] for e in range(N_EXPERTS)]
    crow = [scores[e:e + 1, :] for e in range(N_EXPERTS)]
    gscore = [_top2_sum(*srow[EXPERTS_PER_GROUP * g:EXPERTS_PER_GROUP * (g + 1)]) for g in range(N_GROUPS)]
    best = jnp.zeros((1, tm), jnp.int32)
    best_v = gscore[0]
    for g in range(1, N_GROUPS):
        upd = gscore[g] > best_v
        best = jnp.where(upd, g, best)
        best_v = jnp.where(upd, gscore[g], best_v)
    cand, cand_score = [], []
    for j in range(EXPERTS_PER_GROUP):
        cs, cc = srow[j], crow[j]
        for g in range(1, N_GROUPS):
            cs = jnp.where(best == g, srow[EXPERTS_PER_GROUP * g + j], cs)
            cc = jnp.where(best == g, crow[EXPERTS_PER_GROUP * g + j], cc)
        cand.append(cs)
        cand_score.append(cc)
    i0 = jnp.zeros((1, tm), jnp.int32)
    v0, w0 = cand[0], cand_score[0]
    for j in range(1, EXPERTS_PER_GROUP):
        upd = cand[j] > v0
        i0 = jnp.where(upd, j, i0)
        v0 = jnp.where(upd, cand[j], v0)
        w0 = jnp.where(upd, cand_score[j], w0)
    i1 = jnp.zeros((1, tm), jnp.int32)
    v1 = jnp.full((1, tm), -jnp.inf, F32)
    w1 = jnp.zeros((1, tm), F32)
    for j in range(EXPERTS_PER_GROUP):
        upd = jnp.logical_and(i0 != j, cand[j] > v1)
        i1 = jnp.where(upd, j, i1)
        v1 = jnp.where(upd, cand[j], v1)
        w1 = jnp.where(upd, cand_score[j], w1)
    wsum = w0 + w1
    first_lo = i0 < i1
    lo = jnp.where(first_lo, i0, i1)
    hi = jnp.where(first_lo, i1, i0)
    g_lo = jnp.where(first_lo, w0, w1) / wsum
    g_hi = jnp.where(first_lo, w1, w0) / wsum
    pair = jnp.where(lo == 0, hi - 1, jnp.where(lo == 1, hi + 1, PAIRS_PER_GROUP - 1))
    return best * PAIRS_PER_GROUP + pair, g_lo, g_hi


def _outproj_kernel(att_ref, g_ref, gprev_ref, gnext_ref, gb_ref, x_ref, m_ref, wo_ref, cw_ref, cb_ref,
                    lng_ref, lnb_ref, rwt_ref, rbias_ref, tri_ref, x1_ref, hx_ref, meta_ref, cnt_ref, run_ref,
                    *, n_tiles):
    i = pl.program_id(1)

    @pl.when(jnp.logical_and(pl.program_id(0) == 0, i == 0))
    def _():
        run_ref[...] = jnp.zeros_like(run_ref)

    g = g_ref[...]
    tm = g.shape[0]
    rows = lax.broadcasted_iota(jnp.int32, g.shape, 0)
    prev_row = jnp.where(i > 0, gprev_ref[SUBLANES - 1:SUBLANES, :], 0.0)
    next_row = jnp.where(i < n_tiles - 1, gnext_ref[0:1, :], 0.0)
    g_m1 = jnp.where(rows == 0, prev_row, pltpu.roll(g, 1, 0))
    g_p1 = jnp.where(rows == tm - 1, next_row, pltpu.roll(g, tm - 1, 0))
    conv = cw_ref[0:1, :] * g_m1 + cw_ref[1:2, :] * g + cw_ref[2:3, :] * g_p1 + cb_ref[...]
    yc = (gb_ref[...] * conv).astype(BF16)
    y = (jnp.dot(att_ref[...], wo_ref[:D_ATT, :], preferred_element_type=F32)
         + jnp.dot(yc, wo_ref[D_ATT:, :], preferred_element_type=F32))
    x1 = _ln(ALPHA * x_ref[...] + m_ref[2:3, :] * y) * lng_ref[...] + lnb_ref[...]
    x1_ref[...] = x1
    h = _ln(x1) * (1.0 + m_ref[4:5, :]) + m_ref[3:4, :]
    bucket, g_lo, g_hi = _router(h, rwt_ref, rbias_ref)
    onehot = lax.broadcasted_iota(jnp.int32, (N_BUCKET_PAD, tm), 0) == bucket
    onehot_f = jnp.where(onehot, 1.0, 0.0)
    csum = jnp.dot(onehot_f.astype(BF16), tri_ref[...], preferred_element_type=F32)
    run = run_ref[...]
    rank = jnp.sum(jnp.where(onehot, csum - 1.0 + run[:, 0:1], 0.0), axis=0, keepdims=True)
    run = run + jnp.sum(onehot_f, axis=1, keepdims=True)
    run_ref[...] = run
    cnt_ref[...] = run
    rows8 = lax.broadcasted_iota(jnp.int32, (SUBLANES, tm), 0)
    meta = jnp.where(rows8 == 0, bucket.astype(F32), jnp.where(rows8 == 1, rank, 0.0))
    meta_ref[...] = meta
    gate_rows = jnp.where(rows8 == 0, g_lo, jnp.where(rows8 == 1, g_hi, 0.0))
    gate_cols = jnp.concatenate([gate_rows, jnp.zeros((LANES - SUBLANES, tm), F32)], axis=0).T
    hx_ref[:, :D_MODEL] = h
    hx_ref[:, D_MODEL:] = gate_cols


def _outproj(att, g, gb, x, mod_l, wo, cw, cb, lng, lnb, rwt, rbias, tri, *, tm, cond_row):
    b, n, d = x.shape
    n_tiles = n // tm
    g8 = g.reshape(b, n // SUBLANES, SUBLANES, D_CONV)
    tpb = tm // SUBLANES
    row = lambda w: pl.BlockSpec((None, tm, w), lambda bi, i: (bi, i, 0))
    full = lambda s: pl.BlockSpec(s, lambda bi, i: (0,) * len(s))
    return pl.pallas_call(
        functools.partial(_outproj_kernel, n_tiles=n_tiles),
        grid=(b, n_tiles),
        in_specs=[
            row(D_ATT), row(D_CONV),
            pl.BlockSpec((None, None, SUBLANES, D_CONV), lambda bi, i: (bi, jnp.maximum(i * tpb - 1, 0), 0, 0)),
            pl.BlockSpec((None, None, SUBLANES, D_CONV),
                         lambda bi, i: (bi, jnp.minimum((i + 1) * tpb, n // SUBLANES - 1), 0, 0)),
            row(D_CONV), row(d),
            pl.BlockSpec((None, N_MOD_PAD, d), lambda bi, i: (cond_row(bi), 0, 0)),
            full((d, d)), full((3, D_CONV)), full((1, D_CONV)), full((1, d)), full((1, d)),
            full((N_EXPERTS, d)), full((N_EXPERTS, 1)), full((tm, tm)),
        ],
        out_specs=[row(d), row(HX_W), pl.BlockSpec((None, SUBLANES, tm), lambda bi, i: (bi, 0, i)),
                   full((N_BUCKET_PAD, LANES))],
        out_shape=[jax.ShapeDtypeStruct((b, n, d), F32), jax.ShapeDtypeStruct((b, n, HX_W), F32),
                   jax.ShapeDtypeStruct((b, SUBLANES, n), F32),
                   jax.ShapeDtypeStruct((N_BUCKET_PAD, LANES), F32)],
        scratch_shapes=[pltpu.VMEM((N_BUCKET_PAD, LANES), F32)],
        compiler_params=_params("arbitrary", "arbitrary"),
        name="outproj_ln1_router",
    )(att, g, g8, g8, gb, x, mod_l, wo, cw, cb, lng, lnb, rwt, rbias, tri)


def _dispatch_kernel(pad_start_ref, pad_len_ref, pos_ref, hx_ref, xs_hbm, zero_ref, sem, zero_sem):
    step = pl.program_id(0)
    chunk = hx_ref.shape[0]

    def issue(g, carry):
        base = pl.multiple_of(g * SUBLANES, SUBLANES)
        for u in range(SUBLANES):
            pltpu.make_async_copy(hx_ref.at[pl.ds(base + u, 1)], xs_hbm.at[pl.ds(pos_ref[0, base + u], 1)],
                                  sem).start()
        return carry

    lax.fori_loop(0, chunk // SUBLANES, issue, 0, unroll=2)

    @pl.when(step == 0)
    def _():
        zero_ref[...] = jnp.zeros_like(zero_ref)

        half = zero_ref.shape[0]
        n_tiles = xs_hbm.shape[0] // (2 * half)
        n_live = pad_start_ref[N_BUCKETS]

        def pad_copies(act):
            zero_copy = lambda start, rows: act(pltpu.make_async_copy(
                zero_ref.at[pl.ds(0, rows)], xs_hbm.at[pl.ds(start, rows)], zero_sem))
            for b in range(N_BUCKETS):
                start, length = pad_start_ref[b], pad_len_ref[b]
                head = jnp.minimum((-start) & (SUBLANES - 1), length)
                for k in range(SUBLANES - 1):
                    @pl.when(k < head)
                    def _(start=start, k=k):
                        zero_copy(start + k, 1)
                start = start + head
                length = length - head
                piece = half
                while piece >= SUBLANES:
                    @pl.when((length & piece) != 0)
                    def _(start=start, piece=piece):
                        zero_copy(pl.multiple_of(start, SUBLANES), piece)
                    start = start + (length & piece)
                    piece //= 2
            for k in range(N_BUCKETS):
                @pl.when(n_tiles - 1 - k >= n_live)
                def _(k=k):
                    zero_copy((n_tiles - 1 - k) * 2 * half, half)
                    zero_copy((n_tiles - 1 - k) * 2 * half + half, half)

        pad_copies(lambda copy: copy.start())
        pad_copies(lambda copy: copy.wait())

    pltpu.make_async_copy(hx_ref, xs_hbm.at[pl.ds(0, chunk)], sem).wait()


def _dispatch(pos, pad_start, pad_len, hx, n_rows, *, tms):
    n, w = hx.shape
    chunk = DISPATCH_CHUNK
    return pl.pallas_call(
        _dispatch_kernel,
        grid_spec=pltpu.PrefetchScalarGridSpec(
            num_scalar_prefetch=2,
            grid=(n // chunk,),
            in_specs=[pl.BlockSpec((None, 1, chunk), lambda i, ps, pn: (i, 0, 0), memory_space=pltpu.SMEM),
                      pl.BlockSpec((chunk, w), lambda i, ps, pn: (i, 0))],
            out_specs=pl.BlockSpec(memory_space=pl.ANY),
            scratch_shapes=[pltpu.VMEM((tms // 2, w), F32), pltpu.SemaphoreType.DMA(()),
                            pltpu.SemaphoreType.DMA(())],
        ),
        out_shape=jax.ShapeDtypeStruct((n_rows, w), F32),
        compiler_params=_params("arbitrary"),
        name="moe_dispatch",
    )(pad_start, pad_len, pos.reshape(n // chunk, 1, chunk), hx)


def _experts_kernel(blk_ref, ea_ref, eb_ref, nv_ref, xs_ref, wga_ref, wua_ref, wda_ref, wgb_ref, wub_ref, wdb_ref,
                    ys_ref):
    nv = nv_ref[pl.program_id(0)]

    @pl.when(nv == 0)
    def _():
        ys_ref[...] = jnp.zeros_like(ys_ref)

    @pl.when(nv > 0)
    def _():
        x = xs_ref[:, :D_MODEL].astype(BF16)
        g_lo = xs_ref[:, D_MODEL:D_MODEL + 1]
        g_hi = xs_ref[:, D_MODEL + 1:D_MODEL + 2]

        def ffn(wg_ref, wu_ref, wd_ref):
            gate = jnp.dot(x, wg_ref[...], preferred_element_type=F32)
            up = jnp.dot(x, wu_ref[...], preferred_element_type=F32)
            he = (gate * _sigmoid(gate) * up).astype(BF16)
            return jnp.dot(he, wd_ref[...], preferred_element_type=F32)

        ys_ref[...] = g_lo * ffn(wga_ref, wua_ref, wda_ref) + g_hi * ffn(wgb_ref, wub_ref, wdb_ref)


def _experts(tile_blk, tile_ea, tile_eb, tile_nv, xs, wg, wu, wd, *, layer, tms):
    n_rows, w = xs.shape
    d = D_MODEL
    n_tiles = n_rows // tms
    wspec = lambda shape, which: pl.BlockSpec(
        (None, None) + shape, lambda j, blk, ea, eb, nv: (layer, (ea, eb)[which][j], 0, 0))
    return pl.pallas_call(
        _experts_kernel,
        grid_spec=pltpu.PrefetchScalarGridSpec(
            num_scalar_prefetch=4,
            grid=(n_tiles,),
            in_specs=[
                pl.BlockSpec((tms, w), lambda j, blk, ea, eb, nv: (blk[j], 0)),
                wspec((d, D_EXPERT), 0), wspec((d, D_EXPERT), 0), wspec((D_EXPERT, d), 0),
                wspec((d, D_EXPERT), 1), wspec((d, D_EXPERT), 1), wspec((D_EXPERT, d), 1),
            ],
            out_specs=pl.BlockSpec((tms, d), lambda j, blk, ea, eb, nv: (j, 0)),
        ),
        out_shape=jax.ShapeDtypeStruct((n_rows, d), F32),
        compiler_params=_params("arbitrary"),
        name="moe_experts",
    )(tile_blk, tile_ea, tile_eb, tile_nv, xs, wg, wu, wd, wg, wu, wd)


def _combine_kernel(pos_ref, pos_next_ref, ys_hbm, x1_ref, m_ref, lng_ref, lnb_ref, o_ref, buf_ref, sem):
    step = pl.program_id(0)
    tm = buf_ref.shape[1]

    def gather(p_ref, slot):
        def issue(g, carry):
            base = pl.multiple_of(g * SUBLANES, SUBLANES)
            for u in range(SUBLANES):
                pltpu.make_async_copy(ys_hbm.at[pl.ds(p_ref[0, base + u], 1)],
                                      buf_ref.at[slot, pl.ds(base + u, 1)], sem.at[slot]).start()
            return carry

        lax.fori_loop(0, tm // SUBLANES, issue, 0, unroll=2)

    @pl.when(step == 0)
    def _():
        gather(pos_ref, 0)

    slot = step % 2

    @pl.when(step + 1 < pl.num_programs(0))
    def _():
        gather(pos_next_ref, 1 - slot)

    pltpu.make_async_copy(ys_hbm.at[pl.ds(0, tm)], buf_ref.at[slot], sem.at[slot]).wait()
    o_ref[...] = _ln(ALPHA * x1_ref[...] + m_ref[5:6, :] * buf_ref[slot]) * lng_ref[...] + lnb_ref[...]


def _combine(pos, ys, x1, mod_l, lng, lnb, *, cond_row):
    n, d = x1.shape
    tm = TM_COMBINE
    n_tiles = n // tm
    vec = pl.BlockSpec((1, d), lambda i: (0, 0))
    pos3 = pos.reshape(n_tiles, 1, tm)
    return pl.pallas_call(
        _combine_kernel,
        grid=(n_tiles,),
        in_specs=[
            pl.BlockSpec((None, 1, tm), lambda i: (i, 0, 0), memory_space=pltpu.SMEM),
            pl.BlockSpec((None, 1, tm), lambda i: (jnp.minimum(i + 1, n_tiles - 1), 0, 0),
                         memory_space=pltpu.SMEM),
            pl.BlockSpec(memory_space=pl.ANY),
            pl.BlockSpec((tm, d), lambda i: (i, 0)),
            pl.BlockSpec((None, N_MOD_PAD, d), lambda i: (cond_row(i), 0, 0)),
            vec, vec,
        ],
        out_specs=pl.BlockSpec((tm, d), lambda i: (i, 0)),
        out_shape=jax.ShapeDtypeStruct((n, d), F32),
        scratch_shapes=[pltpu.VMEM((2, tm, d), F32), pltpu.SemaphoreType.DMA((2,))],
        compiler_params=_params("arbitrary"),
        name="moe_combine_ln2",
    )(pos3, pos3, ys, x1, mod_l, lng, lnb)


def _routing_plan(meta, counts, *, tms):
    b, _, n = meta.shape
    n_tok = b * n
    bucket = meta[:, 0, :].reshape(n_tok).astype(jnp.int32)
    rank = meta[:, 1, :].reshape(n_tok).astype(jnp.int32)
    cnt = counts[:N_BUCKETS, 0].astype(jnp.int32)
    tiles = (cnt + tms - 1) // tms
    tile_end = jnp.cumsum(tiles)
    tile_off = tile_end - tiles
    n_tiles = n_tok // tms + N_BUCKETS
    ids = jnp.arange(N_BUCKETS, dtype=jnp.int32)
    pos = rank + jnp.sum(jnp.where(bucket[:, None] == ids[None, :], (tile_off * tms)[None, :], 0), axis=1)
    j = jnp.arange(n_tiles, dtype=jnp.int32)
    live = j < tile_end[-1]
    jb = jnp.minimum(j, tile_end[-1] - 1)
    tb = jnp.sum((jb[:, None] >= tile_end[None, :]).astype(jnp.int32), axis=1)
    pick = lambda table: jnp.sum(jnp.where(tb[:, None] == ids[None, :], table[None, :], 0), axis=1)
    nv = jnp.where(live, jnp.clip(pick(cnt) - (jb - pick(tile_off)) * tms, 0, tms), 0)
    pair = ids % PAIRS_PER_GROUP
    lo = jnp.where(pair < 3, 0, jnp.where(pair < 5, 1, 2))
    hi = jnp.where(pair < 3, pair + 1, jnp.where(pair < 5, pair - 1, 3))
    group = ids // PAIRS_PER_GROUP
    ea = pick(group * EXPERTS_PER_GROUP + lo)
    eb = pick(group * EXPERTS_PER_GROUP + hi)
    pad_start = jnp.concatenate([tile_off * tms + cnt, tile_end[-1:]]).astype(jnp.int32)
    pad_len = jnp.concatenate([tiles * tms - cnt, jnp.zeros((1,), jnp.int32)]).astype(jnp.int32)
    return pos, (pad_start, pad_len), (jb, ea, eb, nv.astype(jnp.int32)), n_tiles * tms


def _moe(hx, meta, counts, x1, moe_w, mod_l, lng, lnb, *, layer, tms, cond_row):
    d = D_MODEL
    pos, pads, tile_tables, n_rows = _routing_plan(meta, counts, tms=tms)
    xs = _dispatch(pos, *pads, hx.reshape(-1, HX_W), n_rows, tms=tms)
    ys = _experts(*tile_tables, xs, *moe_w, layer=layer, tms=tms)
    return _combine(pos, ys, x1.reshape(-1, d), mod_l, lng, lnb, cond_row=cond_row)


def _rope_tables():
    t = jnp.arange(SEQ, dtype=jnp.int32)
    pos = jnp.stack([(t // GRID_W).astype(F32), (t % GRID_W).astype(F32)], axis=1)
    inv = 1.0 / (ROPE_THETA ** (jnp.arange(N_FREQ, dtype=F32) / N_FREQ))
    ang = pos[:, :, None] * inv
    cos, sin = jnp.cos(ang), jnp.sin(ang)
    zero = jnp.zeros_like(sin)
    expand = lambda first, second: jnp.tile(
        jnp.stack([first, second], axis=2).reshape(SEQ, DQK), (1, LANES // DQK))
    return expand(cos, cos), expand(-sin, zero), expand(zero, sin)


def kernel(x, c, ctx, c_ctx, w_mod, b_mod, w_in, diff_lambda, attn_norm_g, conv_w, conv_b, w_out,
           ln1_g, ln1_b, ln2_g, ln2_b, router_w, router_bias, w_gate, w_up, w_down):
    assert x.shape == (BATCH, SEQ, D_MODEL) and ctx.shape == (BATCH, CTX_LEN, D_MODEL)
    d = D_MODEL
    cc = jnp.concatenate([c, c_ctx[None, :], jnp.zeros((N_COND_PAD - N_COND, d), F32)], axis=0)
    mod = _modulation(cc, w_mod, b_mod)
    rope_tabs = _rope_tables()
    w_in_b, w_out_b = w_in.astype(BF16), w_out.astype(BF16)
    wg_b, wu_b, wd_b = w_gate.astype(BF16), w_up.astype(BF16), w_down.astype(BF16)
    rwt = router_w.T
    rbias = router_bias.reshape(N_EXPERTS, 1)

    lat_row = lambda bi: bi
    ctx_row = lambda bi: CTX_ROW
    lat_tile_row = lambda i: i // (SEQ // TM_COMBINE)
    tri = lambda tm: (jnp.arange(tm)[:, None] <= jnp.arange(tm)[None, :]).astype(BF16)

    for l in range(DEPTH):
        last = l == DEPTH - 1
        lam_init = 0.8 - 0.6 * math.exp(-0.3 * l)
        vec = lambda p, w: p[l].reshape(1, w)
        post = (w_out_b[l], conv_w[l], vec(conv_b, D_CONV), vec(ln1_g, d), vec(ln1_b, d), rwt, rbias)
        ln2 = (vec(ln2_g, d), vec(ln2_b, d))

        q, kt, v, g, gb = _inproj(x, mod[l], w_in_b[l], rope_tabs, tm=TM_LAT, cond_row=lat_row)
        if last:
            ktc, vc = _inproj(ctx, mod[l], w_in_b[l][:, D_ATT:3 * D_ATT], None, tm=TM_CTX, cond_row=ctx_row,
                              kv_only=True)
        else:
            qc, ktc, vc, gc, gbc = _inproj(ctx, mod[l], w_in_b[l], None, tm=TM_CTX, cond_row=ctx_row)

        att = _attention(diff_lambda[l], q, [(kt, v), (ktc, vc)], attn_norm_g[l], lam_init=lam_init)
        x1, hx, meta, counts = _outproj(att, g, gb, x, mod[l], *post, tri(TM_LAT), tm=TM_LAT, cond_row=lat_row)
        moe_w = (wg_b, wu_b, wd_b)
        x = _moe(hx, meta, counts, x1, moe_w, mod[l], *ln2, layer=l, tms=TMS_LAT, cond_row=lat_tile_row
                 ).reshape(BATCH, SEQ, d)
        if not last:
            att_c = _attention(diff_lambda[l], qc, [(ktc, vc)], attn_norm_g[l], lam_init=lam_init)
            c1, hxc, meta_c, counts_c = _outproj(att_c, gc, gbc, ctx, mod[l], *post, tri(TM_CTX), tm=TM_CTX,
                                                 cond_row=ctx_row)
            ctx = _moe(hxc, meta_c, counts_c, c1, moe_w, mod[l], *ln2, layer=l, tms=TMS_CTX, cond_row=ctx_row
                       ).reshape(BATCH, CTX_LEN, d)
    return x
```
